```python
import jax, jax.numpy as jnp
from jax import lax
import numpy as np

D_MODEL = 2048
BATCH = 1
SEQ = 8192
DEPTH = 2
DEC_BATCH = 128
DEC_SEQ = 1
PAST_LEN = 2048
PAGE_SIZE = 128

HEAD_DIM = 128
MIX_WIDTH = D_MODEL
H_RET = MIX_WIDTH // HEAD_DIM // 2
H_FOX = MIX_WIDTH // HEAD_DIM - H_RET
W_RET = H_RET * HEAD_DIM
W_FOX = H_FOX * HEAD_DIM
IN_WIDTH = 4 * W_RET + 3 * W_FOX + H_FOX
RET_CHUNK = 128
Q_BLOCK = 128
ROPE_BASE = 10000.0
N_GROUPS = 4
EXPERTS_PER_GROUP = 4
N_EXPERTS = N_GROUPS * EXPERTS_PER_GROUP
TOP_K_IN_GROUP = 2
D_EXPERT = D_MODEL // 2
EPS = 1e-6
FORGET_BIAS_INIT = 3.0

kernel_name = 'hybrid_retention_fox_hmoe_step'

F32 = jnp.float32


def rmsnorm(x, w):
    xf = x.astype(F32)
    y = xf * lax.rsqrt(jnp.mean(xf * xf, axis=-1, keepdims=True) + EPS)
    return (y * w.astype(F32)).astype(x.dtype)


def rope(x, pos):
    inv_freq = ROPE_BASE ** (-jnp.arange(0, HEAD_DIM, 2, dtype=F32) / HEAD_DIM)
    ang = pos.astype(F32)[:, None] * inv_freq[None, :]
    cos = jnp.cos(ang)[None, :, None, :]
    sin = jnp.sin(ang)[None, :, None, :]
    xf = x.astype(F32)
    x1, x2 = xf[..., :HEAD_DIM // 2], xf[..., HEAD_DIM // 2:]
    return jnp.concatenate([x1 * cos - x2 * sin, x1 * sin + x2 * cos], axis=-1)


def retention_log_gamma():
    return jnp.log1p(-jnp.exp2(-5.0 - jnp.arange(H_RET, dtype=F32)))


def project(xn, w_in, b_fgt, pos):
    b, s = xn.shape[0], xn.shape[1]
    p = xn @ w_in
    cuts = [W_RET, 2 * W_RET, 3 * W_RET, 4 * W_RET,
            4 * W_RET + W_FOX, 4 * W_RET + 2 * W_FOX, 4 * W_RET + 3 * W_FOX]
    q_r, k_r, v_r, g_r, q_f, k_f, v_f, f_logit = jnp.split(p, cuts, axis=-1)
    q_r = rope(q_r.reshape(b, s, H_RET, HEAD_DIM), pos)
    k_r = rope(k_r.reshape(b, s, H_RET, HEAD_DIM), pos) * (HEAD_DIM ** -0.5)
    v_r = v_r.reshape(b, s, H_RET, HEAD_DIM).astype(F32)
    q_f = q_f.reshape(b, s, H_FOX, HEAD_DIM)
    k_f = k_f.reshape(b, s, H_FOX, HEAD_DIM)
    v_f = v_f.reshape(b, s, H_FOX, HEAD_DIM)
    logf = jax.nn.log_sigmoid((f_logit + b_fgt).astype(F32))
    return q_r, k_r, v_r, g_r, q_f, k_f, v_f, logf


def retention_block(q, k, v, state, log_gamma):
    c = q.shape[2]
    idx = jnp.arange(c, dtype=F32)
    diff = idx[:, None] - idx[None, :]
    lg = log_gamma[:, None, None]
    decay = jnp.where(diff >= 0, jnp.exp(lg * jnp.maximum(diff, 0.0)), 0.0)
    scores = jnp.einsum('bhid,bhjd->bhij', q, k) * decay[None]
    o = jnp.einsum('bhij,bhje->bhie', scores, v)
    xi = jnp.exp(log_gamma[:, None] * (idx[None, :] + 1.0))
    o = o + jnp.einsum('bhid,bhde->bhie', q, state) * xi[None, :, :, None]
    zeta = jnp.exp(log_gamma[:, None] * (c - 1.0 - idx[None, :]))
    new_state = (jnp.exp(log_gamma * c)[None, :, None, None] * state
                 + jnp.einsum('bhjd,bhje->bhde', k * zeta[None, :, :, None], v))
    return o, new_state


def retention_prompt(q, k, v, log_gamma):
    b, s = q.shape[0], q.shape[1]
    nc = s // RET_CHUNK

    def to_chunks(t):
        return t.reshape(b, nc, RET_CHUNK, H_RET, HEAD_DIM).transpose(1, 0, 3, 2, 4)

    def step(state, qkv):
        o, state = retention_block(qkv[0], qkv[1], qkv[2], state, log_gamma)
        return state, o

    s0 = jnp.zeros((b, H_RET, HEAD_DIM, HEAD_DIM), F32)
    s_fin, o = lax.scan(step, s0, (to_chunks(q), to_chunks(k), to_chunks(v)))
    o = o.transpose(1, 0, 3, 2, 4).reshape(b, s, H_RET, HEAD_DIM)
    return o, s_fin


def retention_sample(q, k, v, state, log_gamma):
    tr = lambda t: t.transpose(0, 2, 1, 3)
    o, new_state = retention_block(tr(q), tr(k), tr(v), state.astype(F32), log_gamma)
    return tr(o), new_state


def fox_attend(q, k, v, cq, ck, q_pos, k_pos):
    s = jnp.einsum('bqhd,bkhd->bhqk', q, k) * (HEAD_DIM ** -0.5)
    s = s + cq.transpose(0, 2, 1)[..., None] - ck.transpose(0, 2, 1)[:, :, None, :]
    mask = k_pos[None, :] <= q_pos[:, None]
    s = jnp.where(mask[None, None], s, -jnp.inf)
    p = jax.nn.softmax(s, axis=-1)
    return jnp.einsum('bhqk,bkhd->bqhd', p, v)


def fox_prompt(q, k, v, logf):
    b, s = q.shape[0], q.shape[1]
    nb = s // Q_BLOCK
    c = jnp.cumsum(logf, axis=1)
    kf, vf = k.astype(F32), v.astype(F32)
    qb = q.astype(F32).reshape(b, nb, Q_BLOCK, H_FOX, HEAD_DIM).transpose(1, 0, 2, 3, 4)
    cqb = c.reshape(b, nb, Q_BLOCK, H_FOX).transpose(1, 0, 2, 3)
    k_pos = jnp.arange(s)

    def one_block(args):
        qi, cqi, bi = args
        q_pos = bi * Q_BLOCK + jnp.arange(Q_BLOCK)
        return fox_attend(qi, kf, vf, cqi, c, q_pos, k_pos)

    o = lax.map(one_block, (qb, cqb, jnp.arange(nb)))
    return o.transpose(1, 0, 2, 3, 4).reshape(b, s, H_FOX, HEAD_DIM)


def fox_sample(q, k, v, logf, ck_pool, cv_pool, clf_pool, page_table):
    db, t = q.shape[0], q.shape[1]
    k_past = ck_pool[page_table].reshape(db, PAST_LEN, H_FOX, HEAD_DIM)
    v_past = cv_pool[page_table].reshape(db, PAST_LEN, H_FOX, HEAD_DIM)
    lf_past = clf_pool[page_table].reshape(db, PAST_LEN, H_FOX)
    k_all = jnp.concatenate([k_past.astype(F32), k.astype(F32)], axis=1)
    v_all = jnp.concatenate([v_past.astype(F32), v.astype(F32)], axis=1)
    c = jnp.cumsum(jnp.concatenate([lf_past.astype(F32), logf], axis=1), axis=1)
    k_pos = jnp.arange(PAST_LEN + t)
    q_pos = PAST_LEN + jnp.arange(t)
    return fox_attend(q.astype(F32), k_all, v_all, c[:, PAST_LEN:], c, q_pos, k_pos)


def merge_heads(o_r, g_r, o_f, gn_w, w_out):
    b, s = o_r.shape[0], o_r.shape[1]
    mu = jnp.mean(o_r, axis=-1, keepdims=True)
    var = jnp.mean(jnp.square(o_r - mu), axis=-1, keepdims=True)
    o_r = ((o_r - mu) * lax.rsqrt(var + EPS)).reshape(b, s, W_RET) * gn_w.astype(F32)
    o_r = jax.nn.silu(g_r.astype(F32)) * o_r
    cat = jnp.concatenate([o_r, o_f.reshape(b, s, W_FOX)], axis=-1).astype(w_out.dtype)
    return cat @ w_out


def hmoe(xn, w_group, b_group, w_expert, b_expert, w1, w3, w2):
    shp = xn.shape
    t = xn.reshape(-1, shp[-1])
    n = t.shape[0]
    p_grp = jax.nn.softmax((t @ w_group + b_group).astype(F32), axis=-1)
    g_idx = jnp.argmax(p_grp, axis=-1)
    g_val = jnp.max(p_grp, axis=-1, keepdims=True)
    e_logit = (t @ w_expert + b_expert).astype(F32).reshape(n, N_GROUPS, EXPERTS_PER_GROUP)
    e_logit = jnp.einsum('ng,nge->ne', jax.nn.one_hot(g_idx, N_GROUPS, dtype=F32), e_logit)
    p_exp = jax.nn.softmax(e_logit, axis=-1)
    top_v, top_i = lax.top_k(p_exp, TOP_K_IN_GROUP)
    wts = g_val * top_v / jnp.sum(top_v, axis=-1, keepdims=True)
    e_id = g_idx[:, None] * EXPERTS_PER_GROUP + top_i
    gate = jnp.sum(jax.nn.one_hot(e_id, N_EXPERTS, dtype=F32) * wts[..., None], axis=1)

    def expert(acc, args):
        w1e, w3e, w2e, ge = args
        h = jax.nn.silu(t @ w1e) * (t @ w3e)
        return acc + (h @ w2e).astype(F32) * ge[:, None], None

    out, _ = lax.scan(expert, jnp.zeros((n, shp[-1]), F32), (w1, w3, w2, gate.T))
    return out.reshape(shp).astype(xn.dtype)


def setup_inputs(seed: int = 0) -> dict:
    key = jax.random.key(seed)
    ks = jax.random.split(key, 24)
    n_pages = PAST_LEN // PAGE_SIZE
    n_pool = (DEC_BATCH * n_pages * 5) // 4

    def nrm(k, shape, scale):
        return jax.random.normal(k, shape, F32) * scale

    x_prompt = nrm(ks[0], (BATCH, SEQ, D_MODEL), 1.0)
    x_sample = nrm(ks[1], (DEC_BATCH, DEC_SEQ, D_MODEL), 1.0)
    cache_k = nrm(ks[2], (DEPTH, n_pool, PAGE_SIZE, H_FOX, HEAD_DIM), 1.0)
    cache_v = nrm(ks[3], (DEPTH, n_pool, PAGE_SIZE, H_FOX, HEAD_DIM), 1.0)
    cache_logf = jax.nn.log_sigmoid(FORGET_BIAS_INIT + nrm(ks[4], (DEPTH, n_pool, PAGE_SIZE, H_FOX), 1.0))
    state_ret = nrm(ks[5], (DEPTH, DEC_BATCH, H_RET, HEAD_DIM, HEAD_DIM), 0.1)
    page_table = jax.random.permutation(ks[6], n_pool)[:DEC_BATCH * n_pages].reshape(DEC_BATCH, n_pages).astype(jnp.int32)
    norm_mix_w = 1.0 + nrm(ks[7], (DEPTH, D_MODEL), 0.02)
    w_in = nrm(ks[8], (DEPTH, D_MODEL, IN_WIDTH), D_MODEL ** -0.5)
    b_fgt = FORGET_BIAS_INIT + nrm(ks[9], (DEPTH, H_FOX), 0.1)
    ret_gn_w = 1.0 + nrm(ks[10], (DEPTH, W_RET), 0.02)
    w_out = nrm(ks[11], (DEPTH, MIX_WIDTH, D_MODEL), MIX_WIDTH ** -0.5)
    norm_ffn_w = 1.0 + nrm(ks[12], (DEPTH, D_MODEL), 0.02)
    w_group = nrm(ks[13], (DEPTH, D_MODEL, N_GROUPS), D_MODEL ** -0.5)
    b_group = nrm(ks[14], (DEPTH, N_GROUPS), 0.01)
    w_expert = nrm(ks[15], (DEPTH, D_MODEL, N_EXPERTS), D_MODEL ** -0.5)
    b_expert = nrm(ks[16], (DEPTH, N_EXPERTS), 0.01)
    w1 = nrm(ks[17], (DEPTH, N_EXPERTS, D_MODEL, D_EXPERT), D_MODEL ** -0.5)
    w3 = nrm(ks[18], (DEPTH, N_EXPERTS, D_MODEL, D_EXPERT), D_MODEL ** -0.5)
    w2 = nrm(ks[19], (DEPTH, N_EXPERTS, D_EXPERT, D_MODEL), D_EXPERT ** -0.5)
    norm_final_w = 1.0 + nrm(ks[20], (D_MODEL,), 0.02)
    return {'x_prompt': x_prompt, 'x_sample': x_sample, 'cache_k': cache_k, 'cache_v': cache_v,
            'cache_logf': cache_logf, 'state_ret': state_ret, 'page_table': page_table,
            'norm_mix_w': norm_mix_w, 'w_in': w_in, 'b_fgt': b_fgt, 'ret_gn_w': ret_gn_w, 'w_out': w_out,
            'norm_ffn_w': norm_ffn_w, 'w_group': w_group, 'b_group': b_group, 'w_expert': w_expert,
            'b_expert': b_expert, 'w1': w1, 'w3': w3, 'w2': w2, 'norm_final_w': norm_final_w}


def reference(x_prompt, x_sample, cache_k, cache_v, cache_logf, state_ret, page_table,
              norm_mix_w, w_in, b_fgt, ret_gn_w, w_out, norm_ffn_w,
              w_group, b_group, w_expert, b_expert, w1, w3, w2, norm_final_w):
    log_gamma = retention_log_gamma()
    pos_p = jnp.arange(x_prompt.shape[1])
    pos_s = PAST_LEN + jnp.arange(x_sample.shape[1])
    xp, xs = x_prompt, x_sample
    k_p, v_p, lf_p, st_p = [], [], [], []
    k_s, v_s, lf_s, st_s = [], [], [], []
    for l in range(DEPTH):
        q_r, k_r, v_r, g_r, q_f, k_f, v_f, logf = project(rmsnorm(xp, norm_mix_w[l]), w_in[l], b_fgt[l], pos_p)
        o_r, s_new = retention_prompt(q_r, k_r, v_r, log_gamma)
        o_f = fox_prompt(q_f, k_f, v_f, logf)
        xp = xp + merge_heads(o_r, g_r, o_f, ret_gn_w[l], w_out[l])
        xp = xp + hmoe(rmsnorm(xp, norm_ffn_w[l]), w_group[l], b_group[l], w_expert[l], b_expert[l],
                       w1[l], w3[l], w2[l])
        k_p.append(k_f); v_p.append(v_f); lf_p.append(logf); st_p.append(s_new)
        q_r, k_r, v_r, g_r, q_f, k_f, v_f, logf = project(rmsnorm(xs, norm_mix_w[l]), w_in[l], b_fgt[l], pos_s)
        o_r, s_new = retention_sample(q_r, k_r, v_r, state_ret[l], log_gamma)
        o_f = fox_sample(q_f, k_f, v_f, logf, cache_k[l], cache_v[l], cache_logf[l], page_table)
        xs = xs + merge_heads(o_r, g_r, o_f, ret_gn_w[l], w_out[l])
        xs = xs + hmoe(rmsnorm(xs, norm_ffn_w[l]), w_group[l], b_group[l], w_expert[l], b_expert[l],
                       w1[l], w3[l], w2[l])
        k_s.append(k_f); v_s.append(v_f); lf_s.append(logf); st_s.append(s_new)
    y_prompt = rmsnorm(xp, norm_final_w)
    y_sample = rmsnorm(xs, norm_final_w)
    return (y_prompt, y_sample, jnp.stack(k_p), jnp.stack(v_p), jnp.stack(lf_p), jnp.stack(st_p),
            jnp.stack(k_s), jnp.stack(v_s), jnp.stack(lf_s), jnp.stack(st_s))
```

```python
import functools

import numpy as np
import jax
import jax.numpy as jnp
from jax import lax
from jax.experimental import pallas as pl
from jax.experimental.pallas import tpu as pltpu

F32 = jnp.float32
BF16 = jnp.bfloat16

HEAD_DIM = 128
EPS = 1e-6
ROPE_BASE = 10000.0
TOP_K_IN_GROUP = 2
LANES = 128
MIB = 1 << 20
NEG_INF = float("-inf")


def _pick(n, cands):
    for c in cands:
        if n % c == 0:
            return c
    raise ValueError(f"no tile for {n} in {cands}")


def _params(sem, vmem_mib):
    return pltpu.CompilerParams(dimension_semantics=sem, vmem_limit_bytes=vmem_mib * MIB)


def _silu(x):
    return x * (1.0 / (1.0 + jnp.exp(-x)))


def _split(a):
    hi = a.astype(BF16)
    return hi, (a - hi.astype(F32)).astype(BF16)


_NN = (((1,), (0,)), ((), ()))
_NT = (((1,), (1,)), ((), ()))


def _dot3(a, b, dims=_NN):
    ah, al = _split(a)
    bh, bl = _split(b)
    dot = lambda u, v: lax.dot_general(u, v, dims, preferred_element_type=F32)
    m = a.shape[0]
    top = dot(jnp.concatenate([ah, al], axis=0), bh)
    return top[:m] + top[m:] + dot(ah, bl)


def _rmsnorm_kernel(x_ref, w_ref, o_ref):
    x = x_ref[...]
    ms = jnp.mean(x * x, axis=-1, keepdims=True)
    o_ref[...] = (x * lax.rsqrt(ms + EPS) * w_ref[...]).astype(o_ref.dtype)


def _rmsnorm(x, w, out_dtype):
    t, d = x.shape
    tm = _pick(t, (640, 320, 256, 128, 64, 8))
    return pl.pallas_call(
        _rmsnorm_kernel,
        grid=(t // tm,),
        in_specs=[pl.BlockSpec((tm, d), lambda i: (i, 0)), pl.BlockSpec((1, d), lambda i: (0, 0))],
        out_specs=pl.BlockSpec((tm, d), lambda i: (i, 0)),
        out_shape=jax.ShapeDtypeStruct((t, d), out_dtype),
        compiler_params=_params(("parallel",), 40),
        name="rmsnorm",
    )(x, w.reshape(1, d))


def _proj_kernel(x_ref, w_ref, *o_refs, scale):
    acc = jnp.dot(x_ref[...], w_ref[...], preferred_element_type=F32)
    if scale != 1.0:
        acc = acc * scale
    for o_ref in o_refs:
        o_ref[...] = acc.astype(o_ref.dtype)


def _proj_rope_kernel(x_ref, w_ref, cos_ref, sin_ref, o_ref, *, heads, kscale):
    j = pl.program_id(0)
    acc = jnp.dot(x_ref[...], w_ref[...], preferred_element_type=F32)
    cos = cos_ref[...]
    sin = sin_ref[...]
    scale = jnp.where(j == 1, kscale, 1.0).astype(F32)
    for h in range(heads):
        sl = slice(h * HEAD_DIM, (h + 1) * HEAD_DIM)
        xh = acc[:, sl]
        r = xh * cos + pltpu.roll(xh, HEAD_DIM // 2, axis=1) * sin
        o_ref[:, sl] = (r * scale).astype(o_ref.dtype)


def _proj_logf_kernel(x_ref, w_ref, b_ref, o_ref, *, n_out):
    z = jnp.dot(x_ref[...], w_ref[...], preferred_element_type=F32) + b_ref[...]
    lf = jnp.minimum(z, 0.0) - jnp.log1p(jnp.exp(-jnp.abs(z)))
    o_ref[...] = lf[:, :n_out]


def _proj(xn, w, out_dtypes, tn, scale=1.0, name="proj"):
    t, d = xn.shape
    n = w.shape[1]
    tm = _pick(t, (640, 320, 256, 128, 64, 8))
    outs = pl.pallas_call(
        functools.partial(_proj_kernel, scale=scale),
        grid=(n // tn, t // tm),
        in_specs=[pl.BlockSpec((tm, d), lambda j, i: (i, 0)), pl.BlockSpec((d, tn), lambda j, i: (0, j))],
        out_specs=[pl.BlockSpec((tm, tn), lambda j, i: (i, j)) for _ in out_dtypes],
        out_shape=[jax.ShapeDtypeStruct((t, n), dt) for dt in out_dtypes],
        compiler_params=_params(("parallel", "parallel"), 48),
        name=name,
    )(xn, w)
    return outs


def _proj_rope(xn, w, cos2, sin2, w_ret):
    t, d = xn.shape
    tm = _pick(t, (640, 320, 256, 128, 64, 8))
    return pl.pallas_call(
        functools.partial(_proj_rope_kernel, heads=w_ret // HEAD_DIM, kscale=HEAD_DIM ** -0.5),
        grid=(2, t // tm),
        in_specs=[pl.BlockSpec((tm, d), lambda j, i: (i, 0)),
                  pl.BlockSpec((d, w_ret), lambda j, i: (0, j)),
                  pl.BlockSpec((tm, HEAD_DIM), lambda j, i: (i, 0)),
                  pl.BlockSpec((tm, HEAD_DIM), lambda j, i: (i, 0))],
        out_specs=pl.BlockSpec((tm, w_ret), lambda j, i: (i, j)),
        out_shape=jax.ShapeDtypeStruct((t, 2 * w_ret), BF16),
        compiler_params=_params(("parallel", "parallel"), 48),
        name="proj_rope",
    )(xn, w, cos2, sin2)


def _proj_logf(xn, w_pad, b_pad, n_out):
    t, d = xn.shape
    tm = _pick(t, (640, 320, 256, 128, 64, 8))
    return pl.pallas_call(
        functools.partial(_proj_logf_kernel, n_out=n_out),
        grid=(t // tm,),
        in_specs=[pl.BlockSpec((tm, d), lambda i: (i, 0)),
                  pl.BlockSpec((d, LANES), lambda i: (0, 0)),
                  pl.BlockSpec((1, LANES), lambda i: (0, 0))],
        out_specs=pl.BlockSpec((tm, n_out), lambda i: (i, 0)),
        out_shape=jax.ShapeDtypeStruct((t, n_out), F32),
        compiler_params=_params(("parallel",), 40),
        name="proj_logf",
    )(xn, w_pad, b_pad)


def _groupnorm_gate(o, g, gnw):
    mu = jnp.mean(o, axis=-1, keepdims=True)
    oc = o - mu
    var = jnp.mean(oc * oc, axis=-1, keepdims=True)
    return _silu(g) * (oc * lax.rsqrt(var + EPS) * gnw)


def _ret_prompt_kernel(q_ref, k_ref, v_ref, g_ref, gnw_ref, decay_ref, xi_ref, zeta_ref,
                       o_ref, state_ref, *, heads, gamma_c):
    c = pl.program_id(0)

    @pl.when(c == 0)
    def _():
        state_ref[...] = jnp.zeros_like(state_ref)

    for h in range(heads):
        sl = slice(h * HEAD_DIM, (h + 1) * HEAD_DIM)
        q = q_ref[:, sl]
        k = k_ref[:, sl]
        v = v_ref[:, sl]
        st = state_ref[h]
        s = lax.dot_general(q, k, (((1,), (1,)), ((), ())), preferred_element_type=F32) * decay_ref[h]
        o = jnp.dot(s.astype(BF16), v.astype(BF16), preferred_element_type=F32)
        o = o + jnp.dot(q, st.astype(BF16), preferred_element_type=F32) * xi_ref[h]
        vz = (v * zeta_ref[h]).astype(BF16)
        kv = lax.dot_general(k, vz, (((0,), (0,)), ((), ())), preferred_element_type=F32)
        state_ref[h] = gamma_c[h] * st + kv
        o_ref[:, sl] = _groupnorm_gate(o, g_ref[:, sl], gnw_ref[:, sl]).astype(o_ref.dtype)


def _retention_tables(heads, chunk):
    lg = np.log1p(-np.exp2(-5.0 - np.arange(heads, dtype=np.float64)))
    idx = np.arange(chunk, dtype=np.float64)
    diff = idx[:, None] - idx[None, :]
    decay = np.where(diff >= 0, np.exp(lg[:, None, None] * np.maximum(diff, 0.0)), 0.0)
    xi = np.exp(lg[:, None] * (idx[None, :] + 1.0))
    zeta = np.exp(lg[:, None] * (chunk - 1.0 - idx[None, :]))
    ones = np.ones((1, 1, HEAD_DIM))
    return (jnp.asarray(decay, F32), jnp.asarray(xi[:, :, None] * ones, F32),
            jnp.asarray(zeta[:, :, None] * ones, F32),
            tuple(float(x) for x in np.exp(lg * chunk)), tuple(float(x) for x in np.exp(lg)))


def _retention_prompt(qk, vg, gnw, s_len, w_ret, chunk):
    heads = w_ret // HEAD_DIM
    decay, xi, zeta, gamma_c, _ = _retention_tables(heads, chunk)
    tbl = pl.BlockSpec((heads, chunk, HEAD_DIM), lambda c: (0, 0, 0))
    return pl.pallas_call(
        functools.partial(_ret_prompt_kernel, heads=heads, gamma_c=gamma_c),
        grid=(s_len // chunk,),
        in_specs=[pl.BlockSpec((chunk, w_ret), lambda c: (c, 0)),
                  pl.BlockSpec((chunk, w_ret), lambda c: (c, 1)),
                  pl.BlockSpec((chunk, w_ret), lambda c: (c, 0)),
                  pl.BlockSpec((chunk, w_ret), lambda c: (c, 1)),
                  pl.BlockSpec((1, w_ret), lambda c: (0, 0)),
                  tbl, tbl, tbl],
        out_specs=[pl.BlockSpec((chunk, w_ret), lambda c: (c, 0)),
                   pl.BlockSpec((heads, HEAD_DIM, HEAD_DIM), lambda c: (0, 0, 0))],
        out_shape=[jax.ShapeDtypeStruct((s_len, w_ret), BF16),
                   jax.ShapeDtypeStruct((heads, HEAD_DIM, HEAD_DIM), F32)],
        compiler_params=_params(("arbitrary",), 40),
        name="retention_prompt",
    )(qk, qk, vg, vg, gnw.reshape(1, w_ret), decay, xi, zeta)


def _ret_sample_kernel(q_ref, k_ref, v_ref, g_ref, cos_ref, sin_ref, gnw_ref, st_ref, o_ref, ns_ref, *,
                       heads, gamma):
    cos = cos_ref[...]
    sin = sin_ref[...]
    rope = lambda a: a * cos + pltpu.roll(a, HEAD_DIM // 2, axis=1) * sin
    q = rope(q_ref[0])
    k = rope(k_ref[0]) * (HEAD_DIM ** -0.5)
    v = v_ref[0]
    qk = jnp.sum(q * k, axis=-1, keepdims=True)
    pad = jnp.zeros((HEAD_DIM - 2 * heads, HEAD_DIM), F32)
    cols = jnp.concatenate([q, k, pad], axis=0).T
    rows = []
    for h in range(heads):
        st = st_ref[0, 0, h]
        qcol = cols[:, h:h + 1]
        kcol = cols[:, heads + h:heads + h + 1]
        vrow = v[h:h + 1, :]
        rows.append(gamma[h] * jnp.sum(st * qcol, axis=0, keepdims=True) + qk[h:h + 1, :] * vrow)
        ns_ref[0, h] = gamma[h] * st + kcol * vrow
    o = jnp.concatenate(rows, axis=0)
    o_ref[0] = _groupnorm_gate(o, g_ref[0], gnw_ref[...]).astype(o_ref.dtype)


def _retention_sample(q3, k3, v3, g3, cos_row, sin_row, gnw, state_ret, layer):
    db, heads, _ = q3.shape
    gamma = _retention_tables(heads, 1)[4]
    vec = pl.BlockSpec((1, heads, HEAD_DIM), lambda b: (b, 0, 0))
    row = pl.BlockSpec((1, HEAD_DIM), lambda b: (0, 0))
    return pl.pallas_call(
        functools.partial(_ret_sample_kernel, heads=heads, gamma=gamma),
        grid=(db,),
        in_specs=[vec, vec, vec, vec, row, row,
                  pl.BlockSpec((heads, HEAD_DIM), lambda b: (0, 0)),
                  pl.BlockSpec((1, 1, heads, HEAD_DIM, HEAD_DIM), lambda b: (layer, b, 0, 0, 0))],
        out_specs=[vec, pl.BlockSpec((1, heads, HEAD_DIM, HEAD_DIM), lambda b: (b, 0, 0, 0))],
        out_shape=[jax.ShapeDtypeStruct((db, heads, HEAD_DIM), F32),
                   jax.ShapeDtypeStruct((db, heads, HEAD_DIM, HEAD_DIM), F32)],
        compiler_params=_params(("parallel",), 40),
        name="retention_sample",
    )(q3, k3, v3, g3, cos_row, sin_row, gnw.reshape(heads, HEAD_DIM), state_ret)


def _fox_prompt_kernel(q_ref, k_ref, v_ref, cq_ref, ck_ref, o_ref, m_ref, l_ref, acc_ref, *, tq):
    qi = pl.program_id(1)
    ki = pl.program_id(2)

    @pl.when(ki == 0)
    def _():
        m_ref[...] = jnp.full_like(m_ref, NEG_INF)
        l_ref[...] = jnp.zeros_like(l_ref)
        acc_ref[...] = jnp.zeros_like(acc_ref)

    def step(masked):
        s = lax.dot_general(q_ref[...], k_ref[...], (((1,), (1,)), ((), ())), preferred_element_type=F32)
        s = s + cq_ref[0] - ck_ref[0]
        if masked:
            row = lax.broadcasted_iota(jnp.int32, (tq, tq), 0)
            col = lax.broadcasted_iota(jnp.int32, (tq, tq), 1)
            s = jnp.where(col <= row, s, NEG_INF)
        m_prev = m_ref[...]
        m_new = jnp.maximum(m_prev, jnp.max(s, axis=-1, keepdims=True))
        alpha = jnp.exp(m_prev - m_new)
        p = jnp.exp(s - m_new)
        l_ref[...] = alpha * l_ref[...] + jnp.sum(p, axis=-1, keepdims=True)
        acc_ref[...] = alpha * acc_ref[...] + jnp.dot(p.astype(BF16), v_ref[...], preferred_element_type=F32)
        m_ref[...] = m_new

    @pl.when(ki < qi)
    def _():
        step(False)

    @pl.when(ki == qi)
    def _():
        step(True)
        o_ref[...] = (acc_ref[...] / l_ref[...]).astype(o_ref.dtype)


def _fox_prompt(qf, kvb, cq, ck, s_len, w_fox):
    heads = w_fox // HEAD_DIM
    tq = _pick(s_len, (512, 256, 128))
    nq = s_len // tq
    return pl.pallas_call(
        functools.partial(_fox_prompt_kernel, tq=tq),
        grid=(heads, nq, nq),
        in_specs=[pl.BlockSpec((tq, HEAD_DIM), lambda h, qi, ki: (qi, h)),
                  pl.BlockSpec((tq, HEAD_DIM), lambda h, qi, ki: (jnp.minimum(ki, qi), h)),
                  pl.BlockSpec((tq, HEAD_DIM), lambda h, qi, ki: (jnp.minimum(ki, qi), heads + h)),
                  pl.BlockSpec((1, tq, 1), lambda h, qi, ki: (h, qi, 0)),
                  pl.BlockSpec((1, 1, tq), lambda h, qi, ki: (h, 0, jnp.minimum(ki, qi)))],
        out_specs=pl.BlockSpec((tq, HEAD_DIM), lambda h, qi, ki: (qi, h)),
        out_shape=jax.ShapeDtypeStruct((s_len, w_fox), BF16),
        scratch_shapes=[pltpu.VMEM((tq, 1), F32), pltpu.VMEM((tq, 1), F32), pltpu.VMEM((tq, HEAD_DIM), F32)],
        compiler_params=_params(("parallel", "parallel", "arbitrary"), 40),
        name="fox_prompt",
    )(qf, kvb, kvb, cq, ck)


def _fox_sample_kernel(pt_ref, q_ref, kn_ref, vn_ref, bias_ref, *refs, heads, pages_per_step):
    del pt_ref
    k_refs = refs[:pages_per_step]
    v_refs = refs[pages_per_step:2 * pages_per_step]
    o_ref, m_ref, l_ref, acc_ref = refs[2 * pages_per_step:]
    jg = pl.program_id(1)
    q = q_ref[0] * (HEAD_DIM ** -0.5)

    @pl.when(jg == 0)
    def _():
        m_ref[...] = jnp.sum(q * kn_ref[0], axis=-1, keepdims=True)
        l_ref[...] = jnp.ones_like(l_ref)
        acc_ref[...] = vn_ref[0]

    rows = k_refs[0].shape[2]
    sub = lax.broadcasted_iota(jnp.int32, (heads, rows), 0)
    lane = lax.broadcasted_iota(jnp.int32, (heads, rows), 1)
    own_head = (lane % heads) == sub
    for g in range(pages_per_step):
        s = _dot3(q, k_refs[g][0, 0], _NT)
        s = s + bias_ref[0, pl.ds(jg * pages_per_step + g, 1), :]
        s = jnp.where(own_head, s, NEG_INF)
        m_prev = m_ref[...]
        m_new = jnp.maximum(m_prev, jnp.max(s, axis=-1, keepdims=True))
        alpha = jnp.exp(m_prev - m_new)
        p = jnp.exp(s - m_new)
        l_ref[...] = alpha * l_ref[...] + jnp.sum(p, axis=-1, keepdims=True)
        acc_ref[...] = alpha * acc_ref[...] + _dot3(p, v_refs[g][0, 0])
        m_ref[...] = m_new

    @pl.when(jg == pl.num_programs(1) - 1)
    def _():
        o_ref[0] = (acc_ref[...] / l_ref[...]).astype(o_ref.dtype)


def _fox_sample(q3, kn3, vn3, bias, cache_k4, cache_v4, pt_flat, layer, n_pages):
    db, heads, _ = q3.shape
    rows = cache_k4.shape[2]
    g = _pick(n_pages, (4, 2, 1))
    vec = pl.BlockSpec((1, heads, HEAD_DIM), lambda b, jg, pt: (b, 0, 0))

    def page_spec(i):
        return pl.BlockSpec((1, 1, rows, HEAD_DIM),
                            lambda b, jg, pt: (layer, pt[b * n_pages + jg * g + i], 0, 0))

    grid_spec = pltpu.PrefetchScalarGridSpec(
        num_scalar_prefetch=1,
        grid=(db, n_pages // g),
        in_specs=[vec, vec, vec, pl.BlockSpec((1, n_pages, rows), lambda b, jg, pt: (b, 0, 0))]
                 + [page_spec(i) for i in range(g)] + [page_spec(i) for i in range(g)],
        out_specs=vec,
        scratch_shapes=[pltpu.VMEM((heads, 1), F32), pltpu.VMEM((heads, 1), F32),
                        pltpu.VMEM((heads, HEAD_DIM), F32)],
    )
    return pl.pallas_call(
        functools.partial(_fox_sample_kernel, heads=heads, pages_per_step=g),
        grid_spec=grid_spec,
        out_shape=jax.ShapeDtypeStruct((db, heads, HEAD_DIM), F32),
        compiler_params=_params(("parallel", "arbitrary"), 40),
        name="fox_sample",
    )(pt_flat, q3, kn3, vn3, bias, *([cache_k4] * g), *([cache_v4] * g))


def _route(logits, n_groups, per_group):
    tm = logits.shape[0]
    lane = lax.broadcasted_iota(jnp.int32, (tm, LANES), 1).astype(F32)
    big = float(4 * LANES)
    first_lane_of = lambda hit: jnp.min(jnp.where(hit, lane, big), axis=-1, keepdims=True)
    gl = jnp.where(lane < n_groups, logits, NEG_INF)
    gexp = jnp.exp(gl - jnp.max(gl, axis=-1, keepdims=True))
    p_grp = gexp / jnp.sum(gexp, axis=-1, keepdims=True)
    g_val = jnp.max(p_grp, axis=-1, keepdims=True)
    g_idx = first_lane_of(p_grp == g_val)
    lo = n_groups + g_idx * per_group
    in_grp = (lane >= lo) & (lane < lo + per_group)
    el = jnp.where(in_grp, logits, NEG_INF)
    eexp = jnp.exp(el - jnp.max(el, axis=-1, keepdims=True))
    p_exp = eexp / jnp.sum(eexp, axis=-1, keepdims=True)
    pe = jnp.where(in_grp, p_exp, -1.0)
    v1 = jnp.max(pe, axis=-1, keepdims=True)
    i1 = first_lane_of(pe == v1)
    pe2 = jnp.where(lane == i1, -1.0, pe)
    v2 = jnp.max(pe2, axis=-1, keepdims=True)
    i2 = first_lane_of(pe2 == v2)
    den = v1 + v2
    return jnp.where(lane == 0, i1 - n_groups,
                     jnp.where(lane == 1, i2 - n_groups,
                               jnp.where(lane == 2, g_val * v1 / den,
                                         jnp.where(lane == 3, g_val * v2 / den, 0.0))))


def _residual_norm_route(x, y, nw, wr, br, n_groups, per_group):
    x1 = x + y
    ms = jnp.mean(x1 * x1, axis=-1, keepdims=True)
    xn = x1 * lax.rsqrt(ms + EPS) * nw
    return x1, xn, _route(_dot3(xn, wr) + br, n_groups, per_group)


def _outproj_kernel(or_ref, of_ref, x_ref, wo_ref, nw_ref, wr_ref, br_ref,
                    x1_ref, xn_ref, route_ref, *, w_ret, n_groups, per_group):
    y = jnp.dot(or_ref[...], wo_ref[:w_ret, :], preferred_element_type=F32)
    y = y + jnp.dot(of_ref[...], wo_ref[w_ret:, :], preferred_element_type=F32)
    x1_ref[...], xn_ref[...], route_ref[...] = _residual_norm_route(
        x_ref[...], y, nw_ref[...], wr_ref[...], br_ref[...], n_groups, per_group)


def _outproj(o_r, o_f, x, w_out_b, nw, w_r, b_r, n_groups, per_group):
    t, d = x.shape
    w_ret = o_r.shape[1]
    w_fox = o_f.shape[1]
    tm = _pick(t, (320, 256, 128, 64, 8))
    row = lambda i: (i, 0)
    fixed = lambda i: (0, 0)
    return pl.pallas_call(
        functools.partial(_outproj_kernel, w_ret=w_ret, n_groups=n_groups, per_group=per_group),
        grid=(t // tm,),
        in_specs=[pl.BlockSpec((tm, w_ret), row), pl.BlockSpec((tm, w_fox), row), pl.BlockSpec((tm, d), row),
                  pl.BlockSpec((w_ret + w_fox, d), fixed), pl.BlockSpec((1, d), fixed),
                  pl.BlockSpec((d, LANES), fixed), pl.BlockSpec((1, LANES), fixed)],
        out_specs=[pl.BlockSpec((tm, d), row), pl.BlockSpec((tm, d), row), pl.BlockSpec((tm, LANES), row)],
        out_shape=[jax.ShapeDtypeStruct((t, d), F32), jax.ShapeDtypeStruct((t, d), F32),
                   jax.ShapeDtypeStruct((t, LANES), F32)],
        compiler_params=_params(("parallel",), 56),
        name="outproj_router",
    )(o_r, o_f, x, w_out_b, nw.reshape(1, d), w_r, b_r)


def _rms(x, nw):
    ms = jnp.mean(x * x, axis=-1, keepdims=True)
    return x * lax.rsqrt(ms + EPS) * nw


def _sample_proj_kernel(x_ref, nw_ref, w_ref, o_ref):
    o_ref[...] = _dot3(_rms(x_ref[...], nw_ref[...]), w_ref[0])


def _sample_logf_kernel(x_ref, nw_ref, w_ref, b_ref, o_ref, *, n_out):
    z = _dot3(_rms(x_ref[...], nw_ref[...]), w_ref[...]) + b_ref[...]
    lf = jnp.minimum(z, 0.0) - jnp.log1p(jnp.exp(-jnp.abs(z)))
    o_ref[...] = lf[:, :n_out]


def _sample_proj(x, nw, w_in, layer, s_len, db, n_cols):
    d = x.shape[1]
    tn = _pick(n_cols, (512, 256, 128))
    return pl.pallas_call(
        _sample_proj_kernel,
        grid=(n_cols // tn,),
        in_specs=[pl.BlockSpec((db, d), lambda j: (s_len // db, 0)),
                  pl.BlockSpec((1, d), lambda j: (0, 0)),
                  pl.BlockSpec((1, d, tn), lambda j: (layer, 0, j))],
        out_specs=pl.BlockSpec((db, tn), lambda j: (0, j)),
        out_shape=jax.ShapeDtypeStruct((db, n_cols), F32),
        compiler_params=_params(("parallel",), 48),
        name="sample_proj",
    )(x, nw.reshape(1, d), w_in)


def _sample_logf(x, nw, w_pad, b_pad, s_len, db, n_out):
    d = x.shape[1]
    return pl.pallas_call(
        functools.partial(_sample_logf_kernel, n_out=n_out),
        grid=(1,),
        in_specs=[pl.BlockSpec((db, d), lambda j: (s_len // db, 0)),
                  pl.BlockSpec((1, d), lambda j: (0, 0)),
                  pl.BlockSpec((d, LANES), lambda j: (0, 0)),
                  pl.BlockSpec((1, LANES), lambda j: (0, 0))],
        out_specs=pl.BlockSpec((db, n_out), lambda j: (0, 0)),
        out_shape=jax.ShapeDtypeStruct((db, n_out), F32),
        compiler_params=_params(("arbitrary",), 40),
        name="sample_logf",
    )(x, nw.reshape(1, d), w_pad, b_pad)


def _sample_outproj_kernel(cat_ref, w_ref, y_ref):
    y_ref[...] = _dot3(cat_ref[...], w_ref[0])


def _sample_outproj(cat, w_out, layer):
    db, width = cat.shape
    d = w_out.shape[2]
    tn = _pick(d, (512, 256, 128))
    return pl.pallas_call(
        _sample_outproj_kernel,
        grid=(d // tn,),
        in_specs=[pl.BlockSpec((db, width), lambda j: (0, 0)),
                  pl.BlockSpec((1, width, tn), lambda j: (layer, 0, j))],
        out_specs=pl.BlockSpec((db, tn), lambda j: (0, j)),
        out_shape=jax.ShapeDtypeStruct((db, d), F32),
        compiler_params=_params(("parallel",), 48),
        name="sample_outproj",
    )(cat, w_out)


def _sample_router_kernel(y_ref, x_ref, nw_ref, wr_ref, br_ref, x1_in, xn_in, route_in,
                          x1_ref, xn_ref, route_ref, *, n_groups, per_group):
    del x1_in, xn_in, route_in
    x1_ref[...], xn_ref[...], route_ref[...] = _residual_norm_route(
        x_ref[...], y_ref[...], nw_ref[...], wr_ref[...], br_ref[...], n_groups, per_group)


def _sample_router(y, x, nw, w_r, b_r, x1, xn2, route, s_len, n_groups, per_group):
    db, d = y.shape
    blk = s_len // db
    fixed = lambda i: (0, 0)
    rows = lambda i: (blk, 0)
    hbm = pl.BlockSpec(memory_space=pl.ANY)
    return pl.pallas_call(
        functools.partial(_sample_router_kernel, n_groups=n_groups, per_group=per_group),
        grid=(1,),
        in_specs=[pl.BlockSpec((db, d), fixed), pl.BlockSpec((db, d), rows), pl.BlockSpec((1, d), fixed),
                  pl.BlockSpec((d, LANES), fixed), pl.BlockSpec((1, LANES), fixed), hbm, hbm, hbm],
        out_specs=[pl.BlockSpec((db, d), rows), pl.BlockSpec((db, d), rows), pl.BlockSpec((db, LANES), rows)],
        out_shape=[jax.ShapeDtypeStruct(x1.shape, F32), jax.ShapeDtypeStruct(xn2.shape, F32),
                   jax.ShapeDtypeStruct(route.shape, F32)],
        input_output_aliases={5: 0, 6: 1, 7: 2},
        compiler_params=_params(("arbitrary",), 40),
        name="sample_router",
    )(y, x, nw.reshape(1, d), w_r, b_r, x1, xn2, route)


def _gather_rows_kernel(src_ref, x_hbm, o_ref, sem, *, rows):
    base = pl.program_id(0) * rows

    def row_copy(r, src_row):
        return pltpu.make_async_copy(x_hbm.at[pl.ds(src_row, 1), :], o_ref.at[pl.ds(r, 1), :], sem)

    def issue(r, carry):
        row_copy(r, src_ref[base + r]).start()
        return carry

    def wait(r, carry):
        row_copy(r, 0).wait()
        return carry

    lax.fori_loop(0, rows, issue, 0)
    lax.fori_loop(0, rows, wait, 0)


def _gather_rows(x, src, n_out, rows):
    d = x.shape[1]
    grid_spec = pltpu.PrefetchScalarGridSpec(
        num_scalar_prefetch=1,
        grid=(n_out // rows,),
        in_specs=[pl.BlockSpec(memory_space=pl.ANY)],
        out_specs=pl.BlockSpec((rows, d), lambda i, s: (i, 0)),
        scratch_shapes=[pltpu.SemaphoreType.DMA(())],
    )
    return pl.pallas_call(
        functools.partial(_gather_rows_kernel, rows=rows),
        grid_spec=grid_spec,
        out_shape=jax.ShapeDtypeStruct((n_out, d), x.dtype),
        compiler_params=_params(("arbitrary",), 40),
        name="moe_gather",
    )(src, x)


def _expert_kernel(te_ref, nu_ref, x_ref, w1_ref, w3_ref, w2_ref, y_ref):
    del te_ref

    @pl.when(pl.program_id(0) < nu_ref[0])
    def _():
        x = x_ref[...].astype(BF16)
        a = jnp.dot(x, w1_ref[0], preferred_element_type=F32)
        b = jnp.dot(x, w3_ref[0], preferred_element_type=F32)
        hmid = (_silu(a) * b).astype(BF16)
        y_ref[...] = jnp.dot(hmid, w2_ref[0], preferred_element_type=F32)

    @pl.when(pl.program_id(0) >= nu_ref[0])
    def _():
        y_ref[...] = jnp.zeros_like(y_ref)


def _experts(x_sorted, w1b, w3b, w2b, tile_expert, n_used, tm):
    p, d = x_sorted.shape
    de = w1b.shape[2]

    def tile(i, te, nu):
        return (jnp.minimum(i, nu[0] - 1), 0)

    def wsel(i, te, nu):
        return (te[i], 0, 0)

    grid_spec = pltpu.PrefetchScalarGridSpec(
        num_scalar_prefetch=2,
        grid=(p // tm,),
        in_specs=[pl.BlockSpec((tm, d), tile), pl.BlockSpec((1, d, de), wsel),
                  pl.BlockSpec((1, d, de), wsel), pl.BlockSpec((1, de, d), wsel)],
        out_specs=pl.BlockSpec((tm, d), lambda i, te, nu: (i, 0)),
    )
    return pl.pallas_call(
        _expert_kernel,
        grid_spec=grid_spec,
        out_shape=jax.ShapeDtypeStruct((p, d), F32),
        compiler_params=_params(("arbitrary",), 56),
        name="moe_experts",
    )(tile_expert, n_used, x_sorted, w1b, w3b, w2b)


def _combine_kernel(pos_ref, x_ref, route_ref, nw_ref, y_hbm, x2_ref, xn_ref, ybuf, sem, *, rows):
    base = pl.program_id(0) * rows

    def row_copy(r, k, src_row):
        return pltpu.make_async_copy(y_hbm.at[pl.ds(src_row, 1), :], ybuf.at[k, pl.ds(r, 1), :], sem)

    def issue(r, carry):
        for k in range(TOP_K_IN_GROUP):
            row_copy(r, k, pos_ref[TOP_K_IN_GROUP * (base + r) + k]).start()
        return carry

    def wait(r, carry):
        for k in range(TOP_K_IN_GROUP):
            row_copy(r, k, 0).wait()
        return carry

    lax.fori_loop(0, rows, issue, 0)
    lax.fori_loop(0, rows, wait, 0)
    route = route_ref[...]
    moe = route[:, 2:3] * ybuf[0] + route[:, 3:4] * ybuf[1]
    x2 = x_ref[...] + moe
    x2_ref[...] = x2
    ms = jnp.mean(x2 * x2, axis=-1, keepdims=True)
    xn_ref[...] = (x2 * lax.rsqrt(ms + EPS) * nw_ref[...]).astype(xn_ref.dtype)


def _combine(x1, route, y_sorted, pos_flat, nw, xn_dtype):
    t, d = x1.shape
    rows = _pick(t, (128, 64, 8))
    grid_spec = pltpu.PrefetchScalarGridSpec(
        num_scalar_prefetch=1,
        grid=(t // rows,),
        in_specs=[pl.BlockSpec((rows, d), lambda i, p: (i, 0)),
                  pl.BlockSpec((rows, LANES), lambda i, p: (i, 0)),
                  pl.BlockSpec((1, d), lambda i, p: (0, 0)),
                  pl.BlockSpec(memory_space=pl.ANY)],
        out_specs=[pl.BlockSpec((rows, d), lambda i, p: (i, 0)), pl.BlockSpec((rows, d), lambda i, p: (i, 0))],
        scratch_shapes=[pltpu.VMEM((TOP_K_IN_GROUP, rows, d), F32), pltpu.SemaphoreType.DMA(())],
    )
    return pl.pallas_call(
        functools.partial(_combine_kernel, rows=rows),
        grid_spec=grid_spec,
        out_shape=[jax.ShapeDtypeStruct((t, d), F32), jax.ShapeDtypeStruct((t, d), xn_dtype)],
        compiler_params=_params(("arbitrary",), 40),
        name="moe_combine",
    )(pos_flat, x1, route, nw.reshape(1, d), y_sorted)


def _routing_tables(e_id, n_experts, tm, n_tiles):
    flat = e_id.reshape(-1)
    onehot = (flat[:, None] == jnp.arange(n_experts, dtype=jnp.int32)[None, :]).astype(jnp.int32)
    csum = jnp.cumsum(onehot, axis=0)
    rank = jnp.sum((csum - onehot) * onehot, axis=1)
    counts = csum[-1]
    padded = ((counts + tm - 1) // tm) * tm
    ends = jnp.cumsum(padded)
    starts = ends - padded
    pos = (jnp.sum(onehot * starts[None, :], axis=1) + rank).astype(jnp.int32)
    token = jnp.arange(flat.shape[0], dtype=jnp.int32) // TOP_K_IN_GROUP
    src = jnp.zeros((n_tiles * tm,), jnp.int32).at[pos].set(token)
    tile_start = jnp.arange(n_tiles, dtype=jnp.int32) * tm
    n_used = (ends[-1] // tm).astype(jnp.int32)
    tile_expert = jnp.sum((tile_start[:, None] >= ends[None, :]).astype(jnp.int32), axis=1)
    last_expert = jnp.sum((jnp.maximum(ends[-1] - 1, 0) >= ends).astype(jnp.int32))
    tile_expert = jnp.minimum(tile_expert, last_expert).astype(jnp.int32)
    return src, pos, tile_expert, n_used.reshape(1)


def kernel(x_prompt, x_sample, cache_k, cache_v, cache_logf, state_ret, page_table, norm_mix_w, w_in, b_fgt,
           ret_gn_w, w_out, norm_ffn_w, w_group, b_group, w_expert, b_expert, w1, w3, w2, norm_final_w):
    assert x_prompt.shape[0] == 1 and x_sample.shape[1] == 1
    s_len, d = x_prompt.shape[1], x_prompt.shape[2]
    db = x_sample.shape[0]
    depth = w_in.shape[0]
    n_pool, page, h_fox = cache_k.shape[1], cache_k.shape[2], cache_k.shape[3]
    h_ret = state_ret.shape[2]
    w_ret, w_fox = h_ret * HEAD_DIM, h_fox * HEAD_DIM
    n_pages = page_table.shape[1]
    past_len = n_pages * page
    n_groups, n_experts = w_group.shape[-1], w_expert.shape[-1]
    per_group = n_experts // n_groups
    t = s_len + db
    chunk = 128
    tm_e = 256
    n_tiles = (TOP_K_IN_GROUP * t + n_experts * (tm_e - 1) + tm_e - 1) // tm_e

    inv_freq = ROPE_BASE ** (-jnp.arange(0, HEAD_DIM, 2, dtype=F32) / HEAD_DIM)
    posv = jnp.concatenate([jnp.arange(s_len), jnp.full((db,), past_len)]).astype(F32)
    ang = posv[:, None] * inv_freq[None, :]
    cos2 = jnp.concatenate([jnp.cos(ang), jnp.cos(ang)], axis=-1)
    sin2 = jnp.concatenate([-jnp.sin(ang), jnp.sin(ang)], axis=-1)

    cache_k4 = cache_k.reshape(depth, n_pool, page * h_fox, HEAD_DIM)
    cache_v4 = cache_v.reshape(depth, n_pool, page * h_fox, HEAD_DIM)
    pt_flat = page_table.reshape(-1)

    x = jnp.concatenate([x_prompt[0], x_sample[:, 0]], axis=0)
    xn = _rmsnorm(x, norm_mix_w[0], BF16)

    k_p, v_p, lf_p, st_p, k_s, v_s, lf_s, st_s = [], [], [], [], [], [], [], []
    for l in range(depth):
        w_in_b = w_in[l].astype(BF16)
        qk = _proj_rope(xn, w_in_b[:, :2 * w_ret], cos2, sin2, w_ret)
        vg, = _proj(xn, w_in_b[:, 2 * w_ret:4 * w_ret], (F32,), w_ret, name="proj_vg")
        c0 = 4 * w_ret
        qf, = _proj(xn, w_in_b[:, c0:c0 + w_fox], (BF16,), w_fox, scale=HEAD_DIM ** -0.5, name="proj_qf")
        kv, kvb = _proj(xn, w_in_b[:, c0 + w_fox:c0 + 3 * w_fox], (F32, BF16), w_fox, name="proj_kv")
        w_f = jnp.zeros((d, LANES), BF16).at[:, :h_fox].set(w_in_b[:, c0 + 3 * w_fox:])
        b_f = jnp.zeros((1, LANES), F32).at[0, :h_fox].set(b_fgt[l])
        logf = _proj_logf(xn, w_f, b_f, h_fox)

        o_r_p, s_fin = _retention_prompt(qk, vg, ret_gn_w[l], s_len, w_ret, chunk)
        c = jnp.cumsum(logf[:s_len], axis=0)
        o_f_p = _fox_prompt(qf, kvb, c.T[:, :, None], c.T[:, None, :], s_len, w_fox)

        nw_mix = norm_mix_w[l]
        p_s = _sample_proj(x, nw_mix, w_in, l, s_len, db, c0 + 3 * w_fox)
        w_f32 = jnp.zeros((d, LANES), F32).at[:, :h_fox].set(w_in[l][:, c0 + 3 * w_fox:])
        logf_s = _sample_logf(x, nw_mix, w_f32, b_f, s_len, db, h_fox)
        heads3 = lambda w0, h: p_s[:, w0:w0 + h * HEAD_DIM].reshape(db, h, HEAD_DIM)
        o_r_s, s_new = _retention_sample(heads3(0, h_ret), heads3(w_ret, h_ret), heads3(2 * w_ret, h_ret),
                                         heads3(3 * w_ret, h_ret), cos2[s_len:s_len + 1], sin2[s_len:s_len + 1],
                                         ret_gn_w[l], state_ret, l)
        lf_past = cache_logf[l][page_table].reshape(db, past_len, h_fox)
        suffix = jnp.flip(jnp.cumsum(jnp.flip(lf_past, axis=1), axis=1), axis=1) - lf_past
        bias = (suffix + logf_s[:, None, :]).reshape(db, n_pages, page * h_fox)
        o_f_s = _fox_sample(heads3(c0, h_fox), heads3(c0 + w_fox, h_fox), heads3(c0 + 2 * w_fox, h_fox), bias,
                            cache_k4, cache_v4, pt_flat, l, n_pages)
        cat_s = jnp.concatenate([o_r_s.reshape(db, w_ret), o_f_s.reshape(db, w_fox)], axis=1)

        o_r = jnp.concatenate([o_r_p, cat_s[:, :w_ret].astype(BF16)], axis=0)
        o_f = jnp.concatenate([o_f_p, cat_s[:, w_ret:].astype(BF16)], axis=0)

        w_r = jnp.zeros((d, LANES), F32).at[:, :n_groups].set(w_group[l]).at[:, n_groups:n_groups + n_experts].set(w_expert[l])
        b_r = jnp.zeros((1, LANES), F32).at[0, :n_groups].set(b_group[l]).at[0, n_groups:n_groups + n_experts].set(b_expert[l])
        x1, xn2, route = _outproj(o_r, o_f, x, w_out[l].astype(BF16), norm_ffn_w[l], w_r, b_r, n_groups, per_group)
        y_s = _sample_outproj(cat_s, w_out, l)
        x1, xn2, route = _sample_router(y_s, x, norm_ffn_w[l], w_r, b_r, x1, xn2, route, s_len, n_groups, per_group)

        e_id = route[:, :TOP_K_IN_GROUP].astype(jnp.int32)
        src, pos, tile_expert, n_used = _routing_tables(e_id, n_experts, tm_e, n_tiles)
        x_sorted = _gather_rows(xn2, src, n_tiles * tm_e, tm_e)
        y_sorted = _experts(x_sorted, w1[l].astype(BF16), w3[l].astype(BF16), w2[l].astype(BF16),
                            tile_expert, n_used, tm_e)
        last = l == depth - 1
        x, xn = _combine(x1, route, y_sorted, pos, norm_final_w if last else norm_mix_w[l + 1],
                         F32 if last else BF16)

        k_p.append(kv[:s_len, :w_fox].reshape(1, s_len, h_fox, HEAD_DIM))
        v_p.append(kv[:s_len, w_fox:].reshape(1, s_len, h_fox, HEAD_DIM))
        lf_p.append(logf[:s_len].reshape(1, s_len, h_fox))
        st_p.append(s_fin.reshape(1, h_ret, HEAD_DIM, HEAD_DIM))
        k_s.append(heads3(c0 + w_fox, h_fox).reshape(db, 1, h_fox, HEAD_DIM))
        v_s.append(heads3(c0 + 2 * w_fox, h_fox).reshape(db, 1, h_fox, HEAD_DIM))
        lf_s.append(logf_s.reshape(db, 1, h_fox))
        st_s.append(s_new)

    y_prompt = xn[:s_len].reshape(1, s_len, d)
    y_sample = xn[s_len:].reshape(db, 1, d)
    return (y_prompt, y_sample, jnp.stack(k_p), jnp.stack(v_p), jnp.stack(lf_p), jnp.stack(st_p),
            jnp.stack(k_s), jnp.stack(v_s), jnp.stack(lf_s), jnp.stack(st_s))
```

```python
import functools

import numpy as np
import jax
import jax.numpy as jnp
from jax import lax
from jax.experimental import pallas as pl
from jax.experimental.pallas import tpu as pltpu

F32 = jnp.float32
BF16 = jnp.bfloat16

HEAD_DIM = 128
EPS = 1e-6
ROPE_BASE = 10000.0
TOP_K_IN_GROUP = 2
LANES = 128
MIB = 1 << 20
NEG_INF = float("-inf")
LOG2E = 1.4426950408889634


def _pick(n, cands):
    for c in cands:
        if n % c == 0:
            return c
    raise ValueError(f"no tile for {n} in {cands}")


def _params(sem, vmem_mib):
    return pltpu.CompilerParams(dimension_semantics=sem, vmem_limit_bytes=vmem_mib * MIB)


def _silu(x):
    return x * (1.0 / (1.0 + jnp.exp(-x)))


def _split(a):
    hi = a.astype(BF16)
    return hi, (a - hi.astype(F32)).astype(BF16)


_NN = (((1,), (0,)), ((), ()))
_NT = (((1,), (1,)), ((), ()))


def _dot3(a, b, dims=_NN):
    ah, al = _split(a)
    bh, bl = _split(b)
    dot = lambda u, v: lax.dot_general(u, v, dims, preferred_element_type=F32)
    m = a.shape[0]
    top = dot(jnp.concatenate([ah, al], axis=0), bh)
    return top[:m] + top[m:] + dot(ah, bl)


def _rmsnorm_kernel(x_ref, w_ref, o_ref):
    x = x_ref[...]
    ms = jnp.mean(x * x, axis=-1, keepdims=True)
    o_ref[...] = (x * lax.rsqrt(ms + EPS) * w_ref[...]).astype(o_ref.dtype)


def _rmsnorm(x, w, out_dtype):
    t, d = x.shape
    tm = _pick(t, (640, 320, 256, 128, 64, 8))
    return pl.pallas_call(
        _rmsnorm_kernel,
        grid=(t // tm,),
        in_specs=[pl.BlockSpec((tm, d), lambda i: (i, 0)), pl.BlockSpec((1, d), lambda i: (0, 0))],
        out_specs=pl.BlockSpec((tm, d), lambda i: (i, 0)),
        out_shape=jax.ShapeDtypeStruct((t, d), out_dtype),
        compiler_params=_params(("parallel",), 40),
        name="rmsnorm",
    )(x, w.reshape(1, d))


def _load_weight_block(w_ref, wb_ref):
    step = 2 * LANES

    @pl.when(pl.program_id(1) == 0)
    def _():
        for c in range(w_ref.shape[1] // step):
            wb_ref[:, c * step:(c + 1) * step] = w_ref[0, c * step:(c + 1) * step, :].T.astype(BF16)


def _proj_kernel(x_ref, w_ref, *refs, scale):
    o_refs, wb_ref = refs[:-1], refs[-1]
    _load_weight_block(w_ref, wb_ref)
    acc = jnp.dot(x_ref[...], wb_ref[...], preferred_element_type=F32)
    if scale != 1.0:
        acc = acc * scale
    for o_ref in o_refs:
        o_ref[...] = acc.astype(o_ref.dtype)


def _proj_rope_kernel(x_ref, w_ref, cos_ref, sin_ref, o_ref, wb_ref, *, heads, kscale):
    j = pl.program_id(0)
    _load_weight_block(w_ref, wb_ref)
    acc = jnp.dot(x_ref[...], wb_ref[...], preferred_element_type=F32)
    cos = cos_ref[...]
    sin = sin_ref[...]
    scale = jnp.where(j == 1, kscale, 1.0).astype(F32)
    for h in range(heads):
        sl = slice(h * HEAD_DIM, (h + 1) * HEAD_DIM)
        xh = acc[:, sl]
        r = xh * cos + pltpu.roll(xh, HEAD_DIM // 2, axis=1) * sin
        o_ref[:, sl] = (r * scale).astype(o_ref.dtype)


def _log_sigmoid(z):
    return jnp.minimum(z, 0.0) - jnp.log1p(jnp.exp(-jnp.abs(z)))


def _proj_logf_kernel(x_ref, w_ref, b_ref, o_ref):
    z = lax.dot_general(x_ref[...], w_ref[0].astype(BF16), _NT, preferred_element_type=F32)
    o_ref[...] = _log_sigmoid(z + b_ref[...])


def _proj(xn, w_t, layer, blk0, nblk, out_dtypes, tn, scale=1.0, name="proj"):
    t, d = xn.shape
    tm = _pick(t, (640, 320, 256, 128, 64, 8))
    outs = pl.pallas_call(
        functools.partial(_proj_kernel, scale=scale),
        grid=(nblk, t // tm),
        in_specs=[pl.BlockSpec((tm, d), lambda j, i: (i, 0)),
                  pl.BlockSpec((1, tn, d), lambda j, i: (layer, blk0 + j, 0))],
        out_specs=[pl.BlockSpec((tm, tn), lambda j, i: (i, j)) for _ in out_dtypes],
        out_shape=[jax.ShapeDtypeStruct((t, nblk * tn), dt) for dt in out_dtypes],
        scratch_shapes=[pltpu.VMEM((d, tn), BF16)],
        compiler_params=_params(("arbitrary", "arbitrary"), 56),
        name=name,
    )(xn, w_t)
    return outs


def _proj_rope(xn, w_t, layer, cos2, sin2, w_ret):
    t, d = xn.shape
    tm = _pick(t, (640, 320, 256, 128, 64, 8))
    return pl.pallas_call(
        functools.partial(_proj_rope_kernel, heads=w_ret // HEAD_DIM, kscale=HEAD_DIM ** -0.5),
        grid=(2, t // tm),
        in_specs=[pl.BlockSpec((tm, d), lambda j, i: (i, 0)),
                  pl.BlockSpec((1, w_ret, d), lambda j, i: (layer, j, 0)),
                  pl.BlockSpec((tm, HEAD_DIM), lambda j, i: (i, 0)),
                  pl.BlockSpec((tm, HEAD_DIM), lambda j, i: (i, 0))],
        out_specs=pl.BlockSpec((tm, w_ret), lambda j, i: (i, j)),
        out_shape=jax.ShapeDtypeStruct((t, 2 * w_ret), BF16),
        scratch_shapes=[pltpu.VMEM((d, w_ret), BF16)],
        compiler_params=_params(("arbitrary", "arbitrary"), 56),
        name="proj_rope",
    )(xn, w_t, cos2, sin2)


def _proj_logf(xn, w_t, layer, row0, bias):
    t, d = xn.shape
    n_out = bias.shape[1]
    tm = _pick(t, (640, 320, 256, 128, 64, 8))
    return pl.pallas_call(
        _proj_logf_kernel,
        grid=(t // tm,),
        in_specs=[pl.BlockSpec((tm, d), lambda i: (i, 0)),
                  pl.BlockSpec((1, n_out, d), lambda i: (layer, row0 // n_out, 0)),
                  pl.BlockSpec((1, n_out), lambda i: (0, 0))],
        out_specs=pl.BlockSpec((tm, n_out), lambda i: (i, 0)),
        out_shape=jax.ShapeDtypeStruct((t, n_out), F32),
        compiler_params=_params(("parallel",), 40),
        name="proj_logf",
    )(xn, w_t, bias)


def _groupnorm_gate(o, g, gnw):
    mu = jnp.mean(o, axis=-1, keepdims=True)
    oc = o - mu
    var = jnp.mean(oc * oc, axis=-1, keepdims=True)
    return _silu(g) * (oc * lax.rsqrt(var + EPS) * gnw)


def _ret_prompt_kernel(q_ref, k_ref, v_ref, g_ref, gnw_ref, decay_ref, xi_ref, zeta_ref,
                       o_ref, state_ref, *, heads, gamma_c):
    c = pl.program_id(0)

    @pl.when(c == 0)
    def _():
        state_ref[...] = jnp.zeros_like(state_ref)

    for h in range(heads):
        sl = slice(h * HEAD_DIM, (h + 1) * HEAD_DIM)
        q = q_ref[:, sl]
        k = k_ref[:, sl]
        v = v_ref[:, sl]
        st = state_ref[h]
        s = lax.dot_general(q, k, (((1,), (1,)), ((), ())), preferred_element_type=F32) * decay_ref[h]
        o = jnp.dot(s.astype(BF16), v.astype(BF16), preferred_element_type=F32)
        o = o + jnp.dot(q, st.astype(BF16), preferred_element_type=F32) * xi_ref[h]
        vz = (v * zeta_ref[h]).astype(BF16)
        kv = lax.dot_general(k, vz, (((0,), (0,)), ((), ())), preferred_element_type=F32)
        state_ref[h] = gamma_c[h] * st + kv
        o_ref[:, sl] = _groupnorm_gate(o, g_ref[:, sl], gnw_ref[:, sl]).astype(o_ref.dtype)


def _retention_tables(heads, chunk):
    lg = np.log1p(-np.exp2(-5.0 - np.arange(heads, dtype=np.float64)))
    idx = np.arange(chunk, dtype=np.float64)
    diff = idx[:, None] - idx[None, :]
    decay = np.where(diff >= 0, np.exp(lg[:, None, None] * np.maximum(diff, 0.0)), 0.0)
    xi = np.exp(lg[:, None] * (idx[None, :] + 1.0))
    zeta = np.exp(lg[:, None] * (chunk - 1.0 - idx[None, :]))
    ones = np.ones((1, 1, HEAD_DIM))
    return (jnp.asarray(decay, F32), jnp.asarray(xi[:, :, None] * ones, F32),
            jnp.asarray(zeta[:, :, None] * ones, F32),
            tuple(float(x) for x in np.exp(lg * chunk)), tuple(float(x) for x in np.exp(lg)))


def _retention_prompt(qk, vg, gnw, s_len, w_ret, chunk):
    heads = w_ret // HEAD_DIM
    decay, xi, zeta, gamma_c, _ = _retention_tables(heads, chunk)
    tbl = pl.BlockSpec((heads, chunk, HEAD_DIM), lambda c: (0, 0, 0))
    return pl.pallas_call(
        functools.partial(_ret_prompt_kernel, heads=heads, gamma_c=gamma_c),
        grid=(s_len // chunk,),
        in_specs=[pl.BlockSpec((chunk, w_ret), lambda c: (c, 0)),
                  pl.BlockSpec((chunk, w_ret), lambda c: (c, 1)),
                  pl.BlockSpec((chunk, w_ret), lambda c: (c, 0)),
                  pl.BlockSpec((chunk, w_ret), lambda c: (c, 1)),
                  pl.BlockSpec((1, w_ret), lambda c: (0, 0)),
                  tbl, tbl, tbl],
        out_specs=[pl.BlockSpec((chunk, w_ret), lambda c: (c, 0)),
                   pl.BlockSpec((heads, HEAD_DIM, HEAD_DIM), lambda c: (0, 0, 0))],
        out_shape=[jax.ShapeDtypeStruct((s_len, w_ret), BF16),
                   jax.ShapeDtypeStruct((heads, HEAD_DIM, HEAD_DIM), F32)],
        compiler_params=_params(("arbitrary",), 40),
        name="retention_prompt",
    )(qk, qk, vg, vg, gnw.reshape(1, w_ret), decay, xi, zeta)


def _ret_sample_kernel(q_ref, k_ref, v_ref, g_ref, cos_ref, sin_ref, gnw_ref, st_ref, o_ref, ns_ref, *,
                       heads, gamma):
    cos = cos_ref[...]
    sin = sin_ref[...]
    rope = lambda a: a * cos + pltpu.roll(a, HEAD_DIM // 2, axis=1) * sin
    q = rope(q_ref[0])
    k = rope(k_ref[0]) * (HEAD_DIM ** -0.5)
    v = v_ref[0]
    qk = jnp.sum(q * k, axis=-1, keepdims=True)
    pad = jnp.zeros((HEAD_DIM - 2 * heads, HEAD_DIM), F32)
    cols = jnp.concatenate([q, k, pad], axis=0).T
    rows = []
    for h in range(heads):
        st = st_ref[0, 0, h]
        qcol = cols[:, h:h + 1]
        kcol = cols[:, heads + h:heads + h + 1]
        vrow = v[h:h + 1, :]
        rows.append(gamma[h] * jnp.sum(st * qcol, axis=0, keepdims=True) + qk[h:h + 1, :] * vrow)
        ns_ref[0, h] = gamma[h] * st + kcol * vrow
    o = jnp.concatenate(rows, axis=0)
    o_ref[0] = _groupnorm_gate(o, g_ref[0], gnw_ref[...]).astype(o_ref.dtype)


def _retention_sample(q3, k3, v3, g3, cos_row, sin_row, gnw, state_ret, layer):
    db, heads, _ = q3.shape
    gamma = _retention_tables(heads, 1)[4]
    vec = pl.BlockSpec((1, heads, HEAD_DIM), lambda b: (b, 0, 0))
    row = pl.BlockSpec((1, HEAD_DIM), lambda b: (0, 0))
    return pl.pallas_call(
        functools.partial(_ret_sample_kernel, heads=heads, gamma=gamma),
        grid=(db,),
        in_specs=[vec, vec, vec, vec, row, row,
                  pl.BlockSpec((heads, HEAD_DIM), lambda b: (0, 0)),
                  pl.BlockSpec((1, 1, heads, HEAD_DIM, HEAD_DIM), lambda b: (layer, b, 0, 0, 0))],
        out_specs=[vec, pl.BlockSpec((1, heads, HEAD_DIM, HEAD_DIM), lambda b: (b, 0, 0, 0))],
        out_shape=[jax.ShapeDtypeStruct((db, heads, HEAD_DIM), F32),
                   jax.ShapeDtypeStruct((db, heads, HEAD_DIM, HEAD_DIM), F32)],
        compiler_params=_params(("parallel",), 40),
        name="retention_sample",
    )(q3, k3, v3, g3, cos_row, sin_row, gnw.reshape(heads, HEAD_DIM), state_ret)


def _fox_prompt_kernel(q_ref, k_ref, v_ref, cq_ref, ck_ref, o_ref, m_ref, acc_ref, *, tq, row_chunks, col_chunks):
    qi = pl.program_id(1)
    tr = tq // row_chunks
    tc = tq // col_chunks
    m_ref[...] = jnp.full_like(m_ref, NEG_INF)
    acc_ref[...] = jnp.zeros_like(acc_ref)
    ones = jnp.ones((tc, HEAD_DIM), BF16)

    def block(ki, masked):
        ck = ck_ref[0, ki]
        for r in range(row_chunks):
            rows = pl.ds(r * tr, tr)
            q = q_ref[rows, :]
            cq = cq_ref[0, rows, :]
            for c in range(tq // tc):
                if masked and c * tc > r * tr + tr - 1:
                    continue
                start = pl.multiple_of(ki * tq + c * tc, tc)
                k = k_ref[pl.ds(start, tc), :]
                v1 = jnp.concatenate([v_ref[pl.ds(start, tc), :], ones], axis=1)
                t = lax.dot_general(q, k, _NT, preferred_element_type=F32) - ck[:, c * tc:(c + 1) * tc]
                if masked and c * tc + tc - 1 > r * tr:
                    row = lax.broadcasted_iota(jnp.int32, (tr, tc), 0) + r * tr
                    col = lax.broadcasted_iota(jnp.int32, (tr, tc), 1) + c * tc
                    t = jnp.where(col <= row, t, NEG_INF)
                m_prev = m_ref[rows, :]
                m_new = jnp.maximum(m_prev, cq + jnp.max(t, axis=-1, keepdims=True))
                p = jnp.exp2(t + (cq - m_new))
                alpha = jnp.exp2(m_prev - m_new)
                acc_ref[rows, :] = alpha * acc_ref[rows, :] + jnp.dot(p.astype(BF16), v1,
                                                                      preferred_element_type=F32)
                m_ref[rows, :] = m_new

    def body(ki, carry):
        block(ki, False)
        return carry

    lax.fori_loop(0, qi, body, 0)
    block(qi, True)
    o_ref[...] = (acc_ref[:, :HEAD_DIM] / acc_ref[:, HEAD_DIM:]).astype(o_ref.dtype)


def _fox_prompt(qf, kvb, cq, ck, s_len, w_fox):
    heads = w_fox // HEAD_DIM
    tq = ck.shape[-1]
    nq = s_len // tq
    return pl.pallas_call(
        functools.partial(_fox_prompt_kernel, tq=tq, row_chunks=2, col_chunks=1),
        grid=(heads, nq),
        in_specs=[pl.BlockSpec((tq, HEAD_DIM), lambda h, qi: (qi, h)),
                  pl.BlockSpec((s_len, HEAD_DIM), lambda h, qi: (0, h)),
                  pl.BlockSpec((s_len, HEAD_DIM), lambda h, qi: (0, heads + h)),
                  pl.BlockSpec((1, tq, 1), lambda h, qi: (h, qi, 0)),
                  pl.BlockSpec((1, nq, 1, tq), lambda h, qi: (h, 0, 0, 0))],
        out_specs=pl.BlockSpec((tq, HEAD_DIM), lambda h, qi: (qi, h)),
        out_shape=jax.ShapeDtypeStruct((s_len, w_fox), BF16),
        scratch_shapes=[pltpu.VMEM((tq, 1), F32), pltpu.VMEM((tq, 2 * HEAD_DIM), F32)],
        compiler_params=_params(("parallel", "arbitrary"), 40),
        name="fox_prompt",
    )(qf, kvb, kvb, cq, ck)


def _fox_sample_kernel(pt_ref, q_ref, kn_ref, vn_ref, bias_ref, *refs, heads, pages_per_step):
    del pt_ref
    k_refs = refs[:pages_per_step]
    v_refs = refs[pages_per_step:2 * pages_per_step]
    o_ref, m_ref, l_ref, acc_ref = refs[2 * pages_per_step:]
    jg = pl.program_id(1)
    q = q_ref[0] * (HEAD_DIM ** -0.5)

    @pl.when(jg == 0)
    def _():
        m_ref[...] = jnp.sum(q * kn_ref[0], axis=-1, keepdims=True)
        l_ref[...] = jnp.ones_like(l_ref)
        acc_ref[...] = vn_ref[0]

    rows = k_refs[0].shape[2]
    sub = lax.broadcasted_iota(jnp.int32, (heads, rows), 0)
    lane = lax.broadcasted_iota(jnp.int32, (heads, rows), 1)
    own_head = (lane % heads) == sub
    s = []
    for g in range(pages_per_step):
        sg = _dot3(q, k_refs[g][0, 0], _NT)
        sg = sg + bias_ref[0, pl.ds(jg * pages_per_step + g, 1), :]
        s.append(jnp.where(own_head, sg, NEG_INF))
    m_prev = m_ref[...]
    m_new = m_prev
    for sg in s:
        m_new = jnp.maximum(m_new, jnp.max(sg, axis=-1, keepdims=True))
    alpha = jnp.exp(m_prev - m_new)
    l_new = alpha * l_ref[...]
    acc = alpha * acc_ref[...]
    for g, sg in enumerate(s):
        p = jnp.exp(sg - m_new)
        l_new = l_new + jnp.sum(p, axis=-1, keepdims=True)
        acc = acc + _dot3(p, v_refs[g][0, 0])
    l_ref[...] = l_new
    acc_ref[...] = acc
    m_ref[...] = m_new

    @pl.when(jg == pl.num_programs(1) - 1)
    def _():
        o_ref[0] = (acc_ref[...] / l_ref[...]).astype(o_ref.dtype)


def _fox_sample(q3, kn3, vn3, bias, cache_k4, cache_v4, pt_flat, layer, n_pages):
    db, heads, _ = q3.shape
    rows = cache_k4.shape[2]
    g = _pick(n_pages, (8, 4, 2, 1))
    vec = pl.BlockSpec((1, heads, HEAD_DIM), lambda b, jg, pt: (b, 0, 0))

    def page_spec(i):
        return pl.BlockSpec((1, 1, rows, HEAD_DIM),
                            lambda b, jg, pt: (layer, pt[b * n_pages + jg * g + i], 0, 0))

    grid_spec = pltpu.PrefetchScalarGridSpec(
        num_scalar_prefetch=1,
        grid=(db, n_pages // g),
        in_specs=[vec, vec, vec, pl.BlockSpec((1, n_pages, rows), lambda b, jg, pt: (b, 0, 0))]
                 + [page_spec(i) for i in range(g)] + [page_spec(i) for i in range(g)],
        out_specs=vec,
        scratch_shapes=[pltpu.VMEM((heads, 1), F32), pltpu.VMEM((heads, 1), F32),
                        pltpu.VMEM((heads, HEAD_DIM), F32)],
    )
    return pl.pallas_call(
        functools.partial(_fox_sample_kernel, heads=heads, pages_per_step=g),
        grid_spec=grid_spec,
        out_shape=jax.ShapeDtypeStruct((db, heads, HEAD_DIM), F32),
        compiler_params=_params(("parallel", "arbitrary"), 48),
        name="fox_sample",
    )(pt_flat, q3, kn3, vn3, bias, *([cache_k4] * g), *([cache_v4] * g))


def _route(logits, n_groups, per_group):
    tm = logits.shape[0]
    lane = lax.broadcasted_iota(jnp.int32, (tm, LANES), 1).astype(F32)
    big = float(4 * LANES)
    first_lane_of = lambda hit: jnp.min(jnp.where(hit, lane, big), axis=-1, keepdims=True)
    gl = jnp.where(lane < n_groups, logits, NEG_INF)
    gexp = jnp.exp(gl - jnp.max(gl, axis=-1, keepdims=True))
    p_grp = gexp / jnp.sum(gexp, axis=-1, keepdims=True)
    g_val = jnp.max(p_grp, axis=-1, keepdims=True)
    g_idx = first_lane_of(p_grp == g_val)
    lo = n_groups + g_idx * per_group
    in_grp = (lane >= lo) & (lane < lo + per_group)
    el = jnp.where(in_grp, logits, NEG_INF)
    eexp = jnp.exp(el - jnp.max(el, axis=-1, keepdims=True))
    p_exp = eexp / jnp.sum(eexp, axis=-1, keepdims=True)
    pe = jnp.where(in_grp, p_exp, -1.0)
    v1 = jnp.max(pe, axis=-1, keepdims=True)
    i1 = first_lane_of(pe == v1)
    pe2 = jnp.where(lane == i1, -1.0, pe)
    v2 = jnp.max(pe2, axis=-1, keepdims=True)
    i2 = first_lane_of(pe2 == v2)
    den = v1 + v2
    return jnp.where(lane == 0, i1 - n_groups,
                     jnp.where(lane == 1, i2 - n_groups,
                               jnp.where(lane == 2, g_val * v1 / den,
                                         jnp.where(lane == 3, g_val * v2 / den, 0.0))))


def _store_packed_rows(xn, xg_ref):
    half = xn.shape[1] // 2
    bits = lambda a: lax.bitcast_convert_type(a.astype(BF16).astype(F32), jnp.uint32)
    word = (bits(xn[:, :half]) >> 16) | (bits(xn[:, half:]) & jnp.uint32(0xFFFF0000))
    for s in range(half // LANES):
        xg_ref[:, s, :] = word[:, s * LANES:(s + 1) * LANES]


def _load_packed_rows(xg_ref):
    word = jnp.concatenate([xg_ref[:, s, :] for s in range(xg_ref.shape[1])], axis=1)
    lo = lax.bitcast_convert_type(word << 16, F32).astype(BF16)
    hi = lax.bitcast_convert_type(word & jnp.uint32(0xFFFF0000), F32).astype(BF16)
    return lo, hi


def _residual_norm_route(x, y, nw, wr, br, n_groups, per_group):
    x1 = x + y
    ms = jnp.mean(x1 * x1, axis=-1, keepdims=True)
    xn = x1 * lax.rsqrt(ms + EPS) * nw
    return x1, xn, _route(_dot3(xn, wr) + br, n_groups, per_group)


def _outproj_kernel(or_ref, of_ref, x_ref, wo_ref, nw_ref, wr_ref, br_ref,
                    x1_ref, xg_ref, route_ref, *, w_ret, n_groups, per_group):
    y = jnp.dot(or_ref[...], wo_ref[:w_ret, :], preferred_element_type=F32)
    y = y + jnp.dot(of_ref[...], wo_ref[w_ret:, :], preferred_element_type=F32)
    x1_ref[...], xn, route_ref[...] = _residual_norm_route(
        x_ref[...], y, nw_ref[...], wr_ref[...], br_ref[...], n_groups, per_group)
    _store_packed_rows(xn, xg_ref)


def _outproj(o_r, o_f, x, w_out_b, nw, w_r, b_r, n_groups, per_group):
    t, d = x.shape
    w_ret = o_r.shape[1]
    w_fox = o_f.shape[1]
    tm = _pick(t, (320, 256, 128, 64, 8))
    row = lambda i: (i, 0)
    fixed = lambda i: (0, 0)
    return pl.pallas_call(
        functools.partial(_outproj_kernel, w_ret=w_ret, n_groups=n_groups, per_group=per_group),
        grid=(t // tm,),
        in_specs=[pl.BlockSpec((tm, w_ret), row), pl.BlockSpec((tm, w_fox), row), pl.BlockSpec((tm, d), row),
                  pl.BlockSpec((w_ret + w_fox, d), fixed), pl.BlockSpec((1, d), fixed),
                  pl.BlockSpec((d, LANES), fixed), pl.BlockSpec((1, LANES), fixed)],
        out_specs=[pl.BlockSpec((tm, d), row), pl.BlockSpec((tm, d // 2 // LANES, LANES), lambda i: (i, 0, 0)),
                   pl.BlockSpec((tm, LANES), row)],
        out_shape=[jax.ShapeDtypeStruct((t, d), F32), jax.ShapeDtypeStruct((t, d // 2 // LANES, LANES), jnp.uint32),
                   jax.ShapeDtypeStruct((t, LANES), F32)],
        compiler_params=_params(("parallel",), 56),
        name="outproj_router",
    )(o_r, o_f, x, w_out_b, nw.reshape(1, d), w_r, b_r)


def _rms(x, nw):
    ms = jnp.mean(x * x, axis=-1, keepdims=True)
    return x * lax.rsqrt(ms + EPS) * nw


def _sample_proj_kernel(x_ref, nw_ref, w_ref, o_ref):
    o_ref[...] = _dot3(_rms(x_ref[...], nw_ref[...]), w_ref[0], _NT)


def _sample_logf_kernel(x_ref, nw_ref, w_ref, b_ref, o_ref):
    z = _dot3(_rms(x_ref[...], nw_ref[...]), w_ref[0], _NT) + b_ref[...]
    o_ref[...] = _log_sigmoid(z)


def _sample_proj(x, nw, w_t, layer, s_len, db, n_cols):
    d = x.shape[1]
    tn = _pick(n_cols, (512, 256, 128))
    return pl.pallas_call(
        _sample_proj_kernel,
        grid=(n_cols // tn,),
        in_specs=[pl.BlockSpec((db, d), lambda j: (s_len // db, 0)),
                  pl.BlockSpec((1, d), lambda j: (0, 0)),
                  pl.BlockSpec((1, tn, d), lambda j: (layer, j, 0))],
        out_specs=pl.BlockSpec((db, tn), lambda j: (0, j)),
        out_shape=jax.ShapeDtypeStruct((db, n_cols), F32),
        compiler_params=_params(("parallel",), 48),
        name="sample_proj",
    )(x, nw.reshape(1, d), w_t)


def _sample_logf(x, nw, w_t, layer, row0, bias, s_len, db):
    d = x.shape[1]
    n_out = bias.shape[1]
    return pl.pallas_call(
        _sample_logf_kernel,
        grid=(1,),
        in_specs=[pl.BlockSpec((db, d), lambda j: (s_len // db, 0)),
                  pl.BlockSpec((1, d), lambda j: (0, 0)),
                  pl.BlockSpec((1, n_out, d), lambda j: (layer, row0 // n_out, 0)),
                  pl.BlockSpec((1, n_out), lambda j: (0, 0))],
        out_specs=pl.BlockSpec((db, n_out), lambda j: (0, 0)),
        out_shape=jax.ShapeDtypeStruct((db, n_out), F32),
        compiler_params=_params(("arbitrary",), 40),
        name="sample_logf",
    )(x, nw.reshape(1, d), w_t, bias)


def _sample_outproj_kernel(cat_ref, w_ref, y_ref):
    y_ref[...] = _dot3(cat_ref[...], w_ref[0])


def _sample_outproj(cat, w_out, layer):
    db, width = cat.shape
    d = w_out.shape[2]
    tn = _pick(d, (512, 256, 128))
    return pl.pallas_call(
        _sample_outproj_kernel,
        grid=(d // tn,),
        in_specs=[pl.BlockSpec((db, width), lambda j: (0, 0)),
                  pl.BlockSpec((1, width, tn), lambda j: (layer, 0, j))],
        out_specs=pl.BlockSpec((db, tn), lambda j: (0, j)),
        out_shape=jax.ShapeDtypeStruct((db, d), F32),
        compiler_params=_params(("parallel",), 48),
        name="sample_outproj",
    )(cat, w_out)


def _sample_router_kernel(y_ref, x_ref, nw_ref, wr_ref, br_ref, x1_in, xg_in, route_in,
                          x1_ref, xg_ref, route_ref, *, n_groups, per_group):
    del x1_in, xg_in, route_in
    x1_ref[...], xn, route_ref[...] = _residual_norm_route(
        x_ref[...], y_ref[...], nw_ref[...], wr_ref[...], br_ref[...], n_groups, per_group)
    _store_packed_rows(xn, xg_ref)


def _sample_router(y, x, nw, w_r, b_r, x1, xg, route, s_len, n_groups, per_group):
    db, d = y.shape
    blk = s_len // db
    fixed = lambda i: (0, 0)
    rows = lambda i: (blk, 0)
    hbm = pl.BlockSpec(memory_space=pl.ANY)
    return pl.pallas_call(
        functools.partial(_sample_router_kernel, n_groups=n_groups, per_group=per_group),
        grid=(1,),
        in_specs=[pl.BlockSpec((db, d), fixed), pl.BlockSpec((db, d), rows), pl.BlockSpec((1, d), fixed),
                  pl.BlockSpec((d, LANES), fixed), pl.BlockSpec((1, LANES), fixed), hbm, hbm, hbm],
        out_specs=[pl.BlockSpec((db, d), rows), pl.BlockSpec((db,) + xg.shape[1:], lambda i: (blk, 0, 0)),
                   pl.BlockSpec((db, LANES), rows)],
        out_shape=[jax.ShapeDtypeStruct(x1.shape, F32), jax.ShapeDtypeStruct(xg.shape, jnp.uint32),
                   jax.ShapeDtypeStruct(route.shape, F32)],
        input_output_aliases={5: 0, 6: 1, 7: 2},
        compiler_params=_params(("arbitrary",), 40),
        name="sample_router",
    )(y, x, nw.reshape(1, d), w_r, b_r, x1, xg, route)


def _gather_rows_kernel(src_ref, x_hbm, o_ref, sem, *, rows):
    base = pl.program_id(0) * rows

    def row_copy(r, src_row):
        return pltpu.make_async_copy(x_hbm.at[pl.ds(src_row, 1)], o_ref.at[pl.ds(r, 1)], sem)

    def issue(r, carry):
        row_copy(r, src_ref[base + r]).start()
        return carry

    def wait(r, carry):
        row_copy(r, 0).wait()
        return carry

    lax.fori_loop(0, rows, issue, 0)
    lax.fori_loop(0, rows, wait, 0)


def _gather_rows(x, src, n_out, rows):
    slab = x.shape[1:]
    grid_spec = pltpu.PrefetchScalarGridSpec(
        num_scalar_prefetch=1,
        grid=(n_out // rows,),
        in_specs=[pl.BlockSpec(memory_space=pl.ANY)],
        out_specs=pl.BlockSpec((rows,) + slab, lambda i, s: (i, 0, 0)),
        scratch_shapes=[pltpu.SemaphoreType.DMA(())],
    )
    return pl.pallas_call(
        functools.partial(_gather_rows_kernel, rows=rows),
        grid_spec=grid_spec,
        out_shape=jax.ShapeDtypeStruct((n_out,) + slab, x.dtype),
        compiler_params=_params(("arbitrary",), 40),
        name="moe_gather",
    )(src, x)


def _expert_kernel(te_ref, nu_ref, x_ref, w1_ref, w3_ref, w2_ref, y_ref):
    del te_ref

    @pl.when(pl.program_id(0) < nu_ref[0])
    def _():
        lo, hi = _load_packed_rows(x_ref)
        half = lo.shape[1]
        up = lambda w_ref: (jnp.dot(lo, w_ref[0, :half, :], preferred_element_type=F32)
                            + jnp.dot(hi, w_ref[0, half:, :], preferred_element_type=F32))
        hmid = (_silu(up(w1_ref)) * up(w3_ref)).astype(BF16)
        y = jnp.dot(hmid, w2_ref[0], preferred_element_type=F32)
        for s in range(y_ref.shape[1]):
            y_ref[:, s, :] = y[:, s * LANES:(s + 1) * LANES]

    @pl.when(pl.program_id(0) >= nu_ref[0])
    def _():
        y_ref[...] = jnp.zeros_like(y_ref)


def _experts(x_sorted, w1b, w3b, w2b, tile_expert, n_used, tm):
    p = x_sorted.shape[0]
    d, de = w1b.shape[1], w1b.shape[2]

    def tile(i, te, nu):
        return (jnp.minimum(i, nu[0] - 1), 0, 0)

    def wsel(i, te, nu):
        return (te[i], 0, 0)

    grid_spec = pltpu.PrefetchScalarGridSpec(
        num_scalar_prefetch=2,
        grid=(p // tm,),
        in_specs=[pl.BlockSpec((tm,) + x_sorted.shape[1:], tile), pl.BlockSpec((1, d, de), wsel),
                  pl.BlockSpec((1, d, de), wsel), pl.BlockSpec((1, de, d), wsel)],
        out_specs=pl.BlockSpec((tm, d // LANES, LANES), lambda i, te, nu: (i, 0, 0)),
    )
    return pl.pallas_call(
        _expert_kernel,
        grid_spec=grid_spec,
        out_shape=jax.ShapeDtypeStruct((p, d // LANES, LANES), F32),
        compiler_params=_params(("arbitrary",), 56),
        name="moe_experts",
    )(tile_expert, n_used, x_sorted, w1b, w3b, w2b)


def _combine_kernel(pos_ref, x_ref, route_ref, nw_ref, y_hbm, x2_ref, xn_ref, ybuf, sem, *, rows):
    base = pl.program_id(0) * rows

    def row_copy(r, k, src_row):
        return pltpu.make_async_copy(y_hbm.at[pl.ds(src_row, 1)], ybuf.at[k, pl.ds(r, 1)], sem)

    def issue(r, carry):
        for k in range(TOP_K_IN_GROUP):
            row_copy(r, k, pos_ref[TOP_K_IN_GROUP * (base + r) + k]).start()
        return carry

    def wait(r, carry):
        for k in range(TOP_K_IN_GROUP):
            row_copy(r, k, 0).wait()
        return carry

    lax.fori_loop(0, rows, issue, 0)
    lax.fori_loop(0, rows, wait, 0)
    route = route_ref[...]
    wt0, wt1 = route[:, 2:3], route[:, 3:4]
    moe = jnp.concatenate([wt0 * ybuf[0, :, s, :] + wt1 * ybuf[1, :, s, :] for s in range(ybuf.shape[2])], axis=1)
    x2 = x_ref[...] + moe
    x2_ref[...] = x2
    ms = jnp.mean(x2 * x2, axis=-1, keepdims=True)
    xn_ref[...] = (x2 * lax.rsqrt(ms + EPS) * nw_ref[...]).astype(xn_ref.dtype)


def _combine(x1, route, y_sorted, pos_flat, nw, xn_dtype):
    t, d = x1.shape
    rows = _pick(t, (128, 64, 8))
    grid_spec = pltpu.PrefetchScalarGridSpec(
        num_scalar_prefetch=1,
        grid=(t // rows,),
        in_specs=[pl.BlockSpec((rows, d), lambda i, p: (i, 0)),
                  pl.BlockSpec((rows, LANES), lambda i, p: (i, 0)),
                  pl.BlockSpec((1, d), lambda i, p: (0, 0)),
                  pl.BlockSpec(memory_space=pl.ANY)],
        out_specs=[pl.BlockSpec((rows, d), lambda i, p: (i, 0)), pl.BlockSpec((rows, d), lambda i, p: (i, 0))],
        scratch_shapes=[pltpu.VMEM((TOP_K_IN_GROUP, rows) + y_sorted.shape[1:], F32), pltpu.SemaphoreType.DMA(())],
    )
    return pl.pallas_call(
        functools.partial(_combine_kernel, rows=rows),
        grid_spec=grid_spec,
        out_shape=[jax.ShapeDtypeStruct((t, d), F32), jax.ShapeDtypeStruct((t, d), xn_dtype)],
        compiler_params=_params(("arbitrary",), 40),
        name="moe_combine",
    )(pos_flat, x1, route, nw.reshape(1, d), y_sorted)


def _routing_tables(e_id, n_experts, tm, n_tiles):
    flat = e_id.reshape(-1)
    onehot = (flat[:, None] == jnp.arange(n_experts, dtype=jnp.int32)[None, :]).astype(jnp.int32)
    csum = jnp.cumsum(onehot, axis=0)
    rank = jnp.sum((csum - onehot) * onehot, axis=1)
    counts = csum[-1]
    padded = ((counts + tm - 1) // tm) * tm
    ends = jnp.cumsum(padded)
    starts = ends - padded
    pos = (jnp.sum(onehot * starts[None, :], axis=1) + rank).astype(jnp.int32)
    token = jnp.arange(flat.shape[0], dtype=jnp.int32) // TOP_K_IN_GROUP
    src = jnp.zeros((n_tiles * tm,), jnp.int32).at[pos].set(token)
    tile_start = jnp.arange(n_tiles, dtype=jnp.int32) * tm
    n_used = (ends[-1] // tm).astype(jnp.int32)
    tile_expert = jnp.sum((tile_start[:, None] >= ends[None, :]).astype(jnp.int32), axis=1)
    last_expert = jnp.sum((jnp.maximum(ends[-1] - 1, 0) >= ends).astype(jnp.int32))
    tile_expert = jnp.minimum(tile_expert, last_expert).astype(jnp.int32)
    return src, pos, tile_expert, n_used.reshape(1)


def kernel(x_prompt, x_sample, cache_k, cache_v, cache_logf, state_ret, page_table, norm_mix_w, w_in, b_fgt,
           ret_gn_w, w_out, norm_ffn_w, w_group, b_group, w_expert, b_expert, w1, w3, w2, norm_final_w):
    assert x_prompt.shape[0] == 1 and x_sample.shape[1] == 1
    s_len, d = x_prompt.shape[1], x_prompt.shape[2]
    db = x_sample.shape[0]
    depth = w_in.shape[0]
    n_pool, page, h_fox = cache_k.shape[1], cache_k.shape[2], cache_k.shape[3]
    h_ret = state_ret.shape[2]
    w_ret, w_fox = h_ret * HEAD_DIM, h_fox * HEAD_DIM
    n_pages = page_table.shape[1]
    past_len = n_pages * page
    n_groups, n_experts = w_group.shape[-1], w_expert.shape[-1]
    per_group = n_experts // n_groups
    t = s_len + db
    chunk = 128
    tq_f = _pick(s_len, (512, 256, 128))
    tm_e = 256
    n_tiles = (TOP_K_IN_GROUP * t + n_experts * (tm_e - 1) + tm_e - 1) // tm_e

    inv_freq = ROPE_BASE ** (-jnp.arange(0, HEAD_DIM, 2, dtype=F32) / HEAD_DIM)
    posv = jnp.concatenate([jnp.arange(s_len), jnp.full((db,), past_len)]).astype(F32)
    ang = posv[:, None] * inv_freq[None, :]
    cos2 = jnp.concatenate([jnp.cos(ang), jnp.cos(ang)], axis=-1)
    sin2 = jnp.concatenate([-jnp.sin(ang), jnp.sin(ang)], axis=-1)

    cache_k4 = cache_k.reshape(depth, n_pool, page * h_fox, HEAD_DIM)
    cache_v4 = cache_v.reshape(depth, n_pool, page * h_fox, HEAD_DIM)
    pt_flat = page_table.reshape(-1)
    assert w_ret == w_fox
    w_t = jnp.swapaxes(w_in, 1, 2)

    x = jnp.concatenate([x_prompt[0], x_sample[:, 0]], axis=0)
    xn = _rmsnorm(x, norm_mix_w[0], BF16)

    k_p, v_p, lf_p, st_p, k_s, v_s, lf_s, st_s = [], [], [], [], [], [], [], []
    for l in range(depth):
        qk = _proj_rope(xn, w_t, l, cos2, sin2, w_ret)
        vg, = _proj(xn, w_t, l, 2, 2, (F32,), w_ret, name="proj_vg")
        c0 = 4 * w_ret
        qf, = _proj(xn, w_t, l, 4, 1, (BF16,), w_ret, scale=HEAD_DIM ** -0.5 * LOG2E, name="proj_qf")
        kv, kvb = _proj(xn, w_t, l, 5, 2, (F32, BF16), w_ret, name="proj_kv")
        b_f = b_fgt[l].reshape(1, h_fox)
        logf = _proj_logf(xn, w_t, l, c0 + 3 * w_fox, b_f)

        o_r_p, s_fin = _retention_prompt(qk, vg, ret_gn_w[l], s_len, w_ret, chunk)
        c2 = (jnp.cumsum(logf[:s_len], axis=0) * LOG2E).T
        o_f_p = _fox_prompt(qf, kvb, c2[:, :, None], c2.reshape(h_fox, s_len // tq_f, 1, tq_f), s_len, w_fox)

        nw_mix = norm_mix_w[l]
        p_s = _sample_proj(x, nw_mix, w_t, l, s_len, db, c0 + 3 * w_fox)
        logf_s = _sample_logf(x, nw_mix, w_t, l, c0 + 3 * w_fox, b_f, s_len, db)
        heads3 = lambda w0, h: p_s[:, w0:w0 + h * HEAD_DIM].reshape(db, h, HEAD_DIM)
        o_r_s, s_new = _retention_sample(heads3(0, h_ret), heads3(w_ret, h_ret), heads3(2 * w_ret, h_ret),
                                         heads3(3 * w_ret, h_ret), cos2[s_len:s_len + 1], sin2[s_len:s_len + 1],
                                         ret_gn_w[l], state_ret, l)
        lf_past = cache_logf[l][page_table].reshape(db, past_len, h_fox)
        suffix = jnp.flip(jnp.cumsum(jnp.flip(lf_past, axis=1), axis=1), axis=1) - lf_past
        bias = (suffix + logf_s[:, None, :]).reshape(db, n_pages, page * h_fox)
        o_f_s = _fox_sample(heads3(c0, h_fox), heads3(c0 + w_fox, h_fox), heads3(c0 + 2 * w_fox, h_fox), bias,
                            cache_k4, cache_v4, pt_flat, l, n_pages)
        cat_s = jnp.concatenate([o_r_s.reshape(db, w_ret), o_f_s.reshape(db, w_fox)], axis=1)

        o_r = jnp.concatenate([o_r_p, cat_s[:, :w_ret].astype(BF16)], axis=0)
        o_f = jnp.concatenate([o_f_p, cat_s[:, w_ret:].astype(BF16)], axis=0)

        w_r = jnp.zeros((d, LANES), F32).at[:, :n_groups].set(w_group[l]).at[:, n_groups:n_groups + n_experts].set(w_expert[l])
        b_r = jnp.zeros((1, LANES), F32).at[0, :n_groups].set(b_group[l]).at[0, n_groups:n_groups + n_experts].set(b_expert[l])
        x1, xg, route = _outproj(o_r, o_f, x, w_out[l].astype(BF16), norm_ffn_w[l], w_r, b_r, n_groups, per_group)
        y_s = _sample_outproj(cat_s, w_out, l)
        x1, xg, route = _sample_router(y_s, x, norm_ffn_w[l], w_r, b_r, x1, xg, route, s_len, n_groups, per_group)

        e_id = route[:, :TOP_K_IN_GROUP].astype(jnp.int32)
        src, pos, tile_expert, n_used = _routing_tables(e_id, n_experts, tm_e, n_tiles)
        x_sorted = _gather_rows(xg, src, n_tiles * tm_e, tm_e)
        y_sorted = _experts(x_sorted, w1[l].astype(BF16), w3[l].astype(BF16), w2[l].astype(BF16),
                            tile_expert, n_used, tm_e)
        last = l == depth - 1
        x, xn = _combine(x1, route, y_sorted, pos, norm_final_w if last else norm_mix_w[l + 1],
                         F32 if last else BF16)

        k_p.append(kv[:s_len, :w_fox].reshape(1, s_len, h_fox, HEAD_DIM))
        v_p.append(kv[:s_len, w_fox:].reshape(1, s_len, h_fox, HEAD_DIM))
        lf_p.append(logf[:s_len].reshape(1, s_len, h_fox))
        st_p.append(s_fin.reshape(1, h_ret, HEAD_DIM, HEAD_DIM))
        k_s.append(heads3(c0 + w_fox, h_fox).reshape(db, 1, h_fox, HEAD_DIM))
        v_s.append(heads3(c0 + 2 * w_fox, h_fox).reshape(db, 1, h_fox, HEAD_DIM))
        lf_s.append(logf_s.reshape(db, 1, h_fox))
        st_s.append(s_new)

    y_prompt = xn[:s_len].reshape(1, s_len, d)
    y_sample = xn[s_len:].reshape(db, 1, d)
    return (y_prompt, y_sample, jnp.stack(k_p), jnp.stack(v_p), jnp.stack(lf_p), jnp.stack(st_p),
            jnp.stack(k_s), jnp.stack(v_s), jnp.stack(lf_s), jnp.stack(st_s))
```

```python
import functools

import numpy as np
import jax
import jax.numpy as jnp
from jax import lax
from jax.experimental import pallas as pl
from jax.experimental.pallas import tpu as pltpu

F32 = jnp.float32
BF16 = jnp.bfloat16

HEAD_DIM = 128
EPS = 1e-6
ROPE_BASE = 10000.0
TOP_K_IN_GROUP = 2
LANES = 128
MIB = 1 << 20
NEG_INF = float("-inf")
LOG2E = 1.4426950408889634


def _pick(n, cands):
    for c in cands:
        if n % c == 0:
            return c
    raise ValueError(f"no tile for {n} in {cands}")


def _params(sem, vmem_mib):
    return pltpu.CompilerParams(dimension_semantics=sem, vmem_limit_bytes=vmem_mib * MIB)


def _silu(x):
    return x * (1.0 / (1.0 + jnp.exp(-x)))


def _split(a):
    hi = a.astype(BF16)
    return hi, (a - hi.astype(F32)).astype(BF16)


_NN = (((1,), (0,)), ((), ()))
_NT = (((1,), (1,)), ((), ()))


def _dot3(a, b, dims=_NN):
    ah, al = _split(a)
    bh, bl = _split(b)
    dot = lambda u, v: lax.dot_general(u, v, dims, preferred_element_type=F32)
    m = a.shape[0]
    top = dot(jnp.concatenate([ah, al], axis=0), bh)
    return top[:m] + top[m:] + dot(ah, bl)


def _rmsnorm_kernel(x_ref, w_ref, o_ref):
    x = x_ref[...]
    ms = jnp.mean(x * x, axis=-1, keepdims=True)
    o_ref[...] = (x * lax.rsqrt(ms + EPS) * w_ref[...]).astype(o_ref.dtype)


def _rmsnorm(x, w, out_dtype):
    t, d = x.shape
    tm = _pick(t, (640, 320, 256, 128, 64, 8))
    return pl.pallas_call(
        _rmsnorm_kernel,
        grid=(t // tm,),
        in_specs=[pl.BlockSpec((tm, d), lambda i: (i, 0)), pl.BlockSpec((1, d), lambda i: (0, 0))],
        out_specs=pl.BlockSpec((tm, d), lambda i: (i, 0)),
        out_shape=jax.ShapeDtypeStruct((t, d), out_dtype),
        compiler_params=_params(("parallel",), 40),
        name="rmsnorm",
    )(x, w.reshape(1, d))


def _load_weight_block(w_ref, wb_ref):
    step = 2 * LANES

    @pl.when(pl.program_id(1) == 0)
    def _():
        for c in range(w_ref.shape[1] // step):
            wb_ref[:, c * step:(c + 1) * step] = w_ref[0, c * step:(c + 1) * step, :].T.astype(BF16)


def _proj_kernel(x_ref, w_ref, *refs, scale):
    o_refs, wb_ref = refs[:-1], refs[-1]
    _load_weight_block(w_ref, wb_ref)
    acc = jnp.dot(x_ref[...], wb_ref[...], preferred_element_type=F32)
    if scale != 1.0:
        acc = acc * scale
    for o_ref in o_refs:
        o_ref[...] = acc.astype(o_ref.dtype)


def _proj_rope_kernel(x_ref, w_ref, cos_ref, sin_ref, o_ref, wb_ref, *, heads, kscale):
    j = pl.program_id(0)
    _load_weight_block(w_ref, wb_ref)
    acc = jnp.dot(x_ref[...], wb_ref[...], preferred_element_type=F32)
    cos = cos_ref[...]
    sin = sin_ref[...]
    scale = jnp.where(j == 1, kscale, 1.0).astype(F32)
    for h in range(heads):
        sl = slice(h * HEAD_DIM, (h + 1) * HEAD_DIM)
        xh = acc[:, sl]
        r = xh * cos + pltpu.roll(xh, HEAD_DIM // 2, axis=1) * sin
        o_ref[:, sl] = (r * scale).astype(o_ref.dtype)


def _log_sigmoid(z):
    return jnp.minimum(z, 0.0) - jnp.log1p(jnp.exp(-jnp.abs(z)))


def _proj_logf_kernel(x_ref, w_ref, b_ref, o_ref):
    z = lax.dot_general(x_ref[...], w_ref[0].astype(BF16), _NT, preferred_element_type=F32)
    o_ref[...] = _log_sigmoid(z + b_ref[...])


def _proj(xn, w_t, layer, blk0, nblk, out_dtypes, tn, t, scale=1.0, name="proj"):
    d = xn.shape[1]
    tm = _pick(t, (512, 256, 128, 64, 8))
    outs = pl.pallas_call(
        functools.partial(_proj_kernel, scale=scale),
        grid=(nblk, t // tm),
        in_specs=[pl.BlockSpec((tm, d), lambda j, i: (i, 0)),
                  pl.BlockSpec((1, tn, d), lambda j, i: (layer, blk0 + j, 0))],
        out_specs=[pl.BlockSpec((tm, tn), lambda j, i: (i, j)) for _ in out_dtypes],
        out_shape=[jax.ShapeDtypeStruct((t, nblk * tn), dt) for dt in out_dtypes],
        scratch_shapes=[pltpu.VMEM((d, tn), BF16)],
        compiler_params=_params(("arbitrary", "arbitrary"), 56),
        name=name,
    )(xn, w_t)
    return outs


def _proj_rope(xn, w_t, layer, cos2, sin2, w_ret, t):
    d = xn.shape[1]
    tm = _pick(t, (512, 256, 128, 64, 8))
    return pl.pallas_call(
        functools.partial(_proj_rope_kernel, heads=w_ret // HEAD_DIM, kscale=HEAD_DIM ** -0.5),
        grid=(2, t // tm),
        in_specs=[pl.BlockSpec((tm, d), lambda j, i: (i, 0)),
                  pl.BlockSpec((1, w_ret, d), lambda j, i: (layer, j, 0)),
                  pl.BlockSpec((tm, HEAD_DIM), lambda j, i: (i, 0)),
                  pl.BlockSpec((tm, HEAD_DIM), lambda j, i: (i, 0))],
        out_specs=pl.BlockSpec((tm, w_ret), lambda j, i: (i, j)),
        out_shape=jax.ShapeDtypeStruct((t, 2 * w_ret), BF16),
        scratch_shapes=[pltpu.VMEM((d, w_ret), BF16)],
        compiler_params=_params(("arbitrary", "arbitrary"), 56),
        name="proj_rope",
    )(xn, w_t, cos2, sin2)


def _proj_logf(xn, w_t, layer, row0, bias, t):
    d = xn.shape[1]
    n_out = bias.shape[1]
    tm = _pick(t, (512, 256, 128, 64, 8))
    return pl.pallas_call(
        _proj_logf_kernel,
        grid=(t // tm,),
        in_specs=[pl.BlockSpec((tm, d), lambda i: (i, 0)),
                  pl.BlockSpec((1, n_out, d), lambda i: (layer, row0 // n_out, 0)),
                  pl.BlockSpec((1, n_out), lambda i: (0, 0))],
        out_specs=pl.BlockSpec((tm, n_out), lambda i: (i, 0)),
        out_shape=jax.ShapeDtypeStruct((t, n_out), F32),
        compiler_params=_params(("parallel",), 40),
        name="proj_logf",
    )(xn, w_t, bias)


def _groupnorm_gate(o, g, gnw):
    mu = jnp.mean(o, axis=-1, keepdims=True)
    oc = o - mu
    var = jnp.mean(oc * oc, axis=-1, keepdims=True)
    return _silu(g) * (oc * lax.rsqrt(var + EPS) * gnw)


def _ret_prompt_kernel(q_ref, k_ref, v_ref, g_ref, gnw_ref, decay_ref, xi_ref, zeta_ref,
                       o_ref, state_ref, *, heads, gamma_c):
    c = pl.program_id(0)

    @pl.when(c == 0)
    def _():
        state_ref[...] = jnp.zeros_like(state_ref)

    for h in range(heads):
        sl = slice(h * HEAD_DIM, (h + 1) * HEAD_DIM)
        q = q_ref[:, sl]
        k = k_ref[:, sl]
        v = v_ref[:, sl]
        st = state_ref[h]
        s = lax.dot_general(q, k, (((1,), (1,)), ((), ())), preferred_element_type=F32) * decay_ref[h]
        o = jnp.dot(s.astype(BF16), v.astype(BF16), preferred_element_type=F32)
        o = o + jnp.dot(q, st.astype(BF16), preferred_element_type=F32) * xi_ref[h]
        vz = (v * zeta_ref[h]).astype(BF16)
        kv = lax.dot_general(k, vz, (((0,), (0,)), ((), ())), preferred_element_type=F32)
        state_ref[h] = gamma_c[h] * st + kv
        o_ref[:, sl] = _groupnorm_gate(o, g_ref[:, sl], gnw_ref[:, sl]).astype(o_ref.dtype)


def _retention_tables(heads, chunk):
    lg = np.log1p(-np.exp2(-5.0 - np.arange(heads, dtype=np.float64)))
    idx = np.arange(chunk, dtype=np.float64)
    diff = idx[:, None] - idx[None, :]
    decay = np.where(diff >= 0, np.exp(lg[:, None, None] * np.maximum(diff, 0.0)), 0.0)
    xi = np.exp(lg[:, None] * (idx[None, :] + 1.0))
    zeta = np.exp(lg[:, None] * (chunk - 1.0 - idx[None, :]))
    ones = np.ones((1, 1, HEAD_DIM))
    return (jnp.asarray(decay, F32), jnp.asarray(xi[:, :, None] * ones, F32),
            jnp.asarray(zeta[:, :, None] * ones, F32),
            tuple(float(x) for x in np.exp(lg * chunk)), tuple(float(x) for x in np.exp(lg)))


def _retention_prompt(qk, vg, gnw, s_len, w_ret, chunk):
    heads = w_ret // HEAD_DIM
    decay, xi, zeta, gamma_c, _ = _retention_tables(heads, chunk)
    tbl = pl.BlockSpec((heads, chunk, HEAD_DIM), lambda c: (0, 0, 0))
    return pl.pallas_call(
        functools.partial(_ret_prompt_kernel, heads=heads, gamma_c=gamma_c),
        grid=(s_len // chunk,),
        in_specs=[pl.BlockSpec((chunk, w_ret), lambda c: (c, 0)),
                  pl.BlockSpec((chunk, w_ret), lambda c: (c, 1)),
                  pl.BlockSpec((chunk, w_ret), lambda c: (c, 0)),
                  pl.BlockSpec((chunk, w_ret), lambda c: (c, 1)),
                  pl.BlockSpec((1, w_ret), lambda c: (0, 0)),
                  tbl, tbl, tbl],
        out_specs=[pl.BlockSpec((chunk, w_ret), lambda c: (c, 0)),
                   pl.BlockSpec((heads, HEAD_DIM, HEAD_DIM), lambda c: (0, 0, 0))],
        out_shape=[jax.ShapeDtypeStruct((s_len, w_ret), BF16),
                   jax.ShapeDtypeStruct((heads, HEAD_DIM, HEAD_DIM), F32)],
        compiler_params=_params(("arbitrary",), 40),
        name="retention_prompt",
    )(qk, qk, vg, vg, gnw.reshape(1, w_ret), decay, xi, zeta)


def _ret_sample_kernel(q_ref, k_ref, v_ref, g_ref, cos_ref, sin_ref, gnw_ref, st_ref, o_ref, ns_ref, *,
                       heads, gamma):
    cos = cos_ref[...]
    sin = sin_ref[...]
    rope = lambda a: a * cos + pltpu.roll(a, HEAD_DIM // 2, axis=1) * sin
    q = rope(q_ref[0])
    k = rope(k_ref[0]) * (HEAD_DIM ** -0.5)
    v = v_ref[0]
    qk = jnp.sum(q * k, axis=-1, keepdims=True)
    pad = jnp.zeros((HEAD_DIM - 2 * heads, HEAD_DIM), F32)
    cols = jnp.concatenate([q, k, pad], axis=0).T
    rows = []
    for h in range(heads):
        st = st_ref[0, 0, h]
        qcol = cols[:, h:h + 1]
        kcol = cols[:, heads + h:heads + h + 1]
        vrow = v[h:h + 1, :]
        rows.append(gamma[h] * jnp.sum(st * qcol, axis=0, keepdims=True) + qk[h:h + 1, :] * vrow)
        ns_ref[0, h] = gamma[h] * st + kcol * vrow
    o = jnp.concatenate(rows, axis=0)
    o_ref[0] = _groupnorm_gate(o, g_ref[0], gnw_ref[...]).astype(o_ref.dtype)


def _retention_sample(q3, k3, v3, g3, cos_row, sin_row, gnw, state_ret, layer):
    db, heads, _ = q3.shape
    gamma = _retention_tables(heads, 1)[4]
    vec = pl.BlockSpec((1, heads, HEAD_DIM), lambda b: (b, 0, 0))
    row = pl.BlockSpec((1, HEAD_DIM), lambda b: (0, 0))
    return pl.pallas_call(
        functools.partial(_ret_sample_kernel, heads=heads, gamma=gamma),
        grid=(db,),
        in_specs=[vec, vec, vec, vec, row, row,
                  pl.BlockSpec((heads, HEAD_DIM), lambda b: (0, 0)),
                  pl.BlockSpec((1, 1, heads, HEAD_DIM, HEAD_DIM), lambda b: (layer, b, 0, 0, 0))],
        out_specs=[vec, pl.BlockSpec((1, heads, HEAD_DIM, HEAD_DIM), lambda b: (b, 0, 0, 0))],
        out_shape=[jax.ShapeDtypeStruct((db, heads, HEAD_DIM), F32),
                   jax.ShapeDtypeStruct((db, heads, HEAD_DIM, HEAD_DIM), F32)],
        compiler_params=_params(("parallel",), 40),
        name="retention_sample",
    )(q3, k3, v3, g3, cos_row, sin_row, gnw.reshape(heads, HEAD_DIM), state_ret)


def _fox_prompt_kernel(q_ref, k_ref, v_ref, cq_ref, ck_ref, o_ref, m_ref, acc_ref, *, tq, row_chunks, col_chunks):
    qi = pl.program_id(1)
    tr = tq // row_chunks
    tc = tq // col_chunks
    m_ref[...] = jnp.full_like(m_ref, NEG_INF)
    acc_ref[...] = jnp.zeros_like(acc_ref)
    ones = jnp.ones((tc, HEAD_DIM), BF16)

    def block(ki, masked):
        ck = ck_ref[0, ki]
        for r in range(row_chunks):
            rows = pl.ds(r * tr, tr)
            q = q_ref[rows, :]
            cq = cq_ref[0, rows, :]
            for c in range(tq // tc):
                if masked and c * tc > r * tr + tr - 1:
                    continue
                start = pl.multiple_of(ki * tq + c * tc, tc)
                k = k_ref[pl.ds(start, tc), :]
                v1 = jnp.concatenate([v_ref[pl.ds(start, tc), :], ones], axis=1)
                t = lax.dot_general(q, k, _NT, preferred_element_type=F32) - ck[:, c * tc:(c + 1) * tc]
                if masked and c * tc + tc - 1 > r * tr:
                    row = lax.broadcasted_iota(jnp.int32, (tr, tc), 0) + r * tr
                    col = lax.broadcasted_iota(jnp.int32, (tr, tc), 1) + c * tc
                    t = jnp.where(col <= row, t, NEG_INF)
                m_prev = m_ref[rows, :]
                m_new = jnp.maximum(m_prev, cq + jnp.max(t, axis=-1, keepdims=True))
                p = jnp.exp2(t + (cq - m_new))
                alpha = jnp.exp2(m_prev - m_new)
                acc_ref[rows, :] = alpha * acc_ref[rows, :] + jnp.dot(p.astype(BF16), v1,
                                                                      preferred_element_type=F32)
                m_ref[rows, :] = m_new

    def body(ki, carry):
        block(ki, False)
        return carry

    lax.fori_loop(0, qi, body, 0)
    block(qi, True)
    o_ref[...] = (acc_ref[:, :HEAD_DIM] / acc_ref[:, HEAD_DIM:]).astype(o_ref.dtype)


def _fox_prompt(qf, kb, vb, cq, ck):
    s_len, w_fox = qf.shape
    heads = w_fox // HEAD_DIM
    tq = ck.shape[-1]
    nq = s_len // tq
    return pl.pallas_call(
        functools.partial(_fox_prompt_kernel, tq=tq, row_chunks=2, col_chunks=1),
        grid=(heads, nq),
        in_specs=[pl.BlockSpec((tq, HEAD_DIM), lambda h, qi: (qi, h)),
                  pl.BlockSpec((s_len, HEAD_DIM), lambda h, qi: (0, h)),
                  pl.BlockSpec((s_len, HEAD_DIM), lambda h, qi: (0, h)),
                  pl.BlockSpec((1, tq, 1), lambda h, qi: (h, qi, 0)),
                  pl.BlockSpec((1, nq, 1, tq), lambda h, qi: (h, 0, 0, 0))],
        out_specs=pl.BlockSpec((tq, HEAD_DIM), lambda h, qi: (qi, h)),
        out_shape=jax.ShapeDtypeStruct((s_len, w_fox), BF16),
        scratch_shapes=[pltpu.VMEM((tq, 1), F32), pltpu.VMEM((tq, 2 * HEAD_DIM), F32)],
        compiler_params=_params(("parallel", "arbitrary"), 40),
        name="fox_prompt",
    )(qf, kb, vb, cq, ck)


def _fox_sample_kernel(pt_ref, q_ref, kn_ref, vn_ref, bias_ref, *refs, heads, pages_per_step):
    del pt_ref
    k_refs = refs[:pages_per_step]
    v_refs = refs[pages_per_step:2 * pages_per_step]
    o_ref, m_ref, l_ref, acc_ref = refs[2 * pages_per_step:]
    jg = pl.program_id(1)
    q = q_ref[0] * (HEAD_DIM ** -0.5)

    @pl.when(jg == 0)
    def _():
        m_ref[...] = jnp.sum(q * kn_ref[0], axis=-1, keepdims=True)
        l_ref[...] = jnp.ones_like(l_ref)
        acc_ref[...] = vn_ref[0]

    rows = k_refs[0].shape[2]
    sub = lax.broadcasted_iota(jnp.int32, (heads, rows), 0)
    lane = lax.broadcasted_iota(jnp.int32, (heads, rows), 1)
    own_head = (lane % heads) == sub
    s = []
    for g in range(pages_per_step):
        sg = _dot3(q, k_refs[g][0, 0], _NT)
        sg = sg + bias_ref[0, pl.ds(jg * pages_per_step + g, 1), :]
        s.append(jnp.where(own_head, sg, NEG_INF))
    m_prev = m_ref[...]
    m_new = m_prev
    for sg in s:
        m_new = jnp.maximum(m_new, jnp.max(sg, axis=-1, keepdims=True))
    alpha = jnp.exp(m_prev - m_new)
    l_new = alpha * l_ref[...]
    acc = alpha * acc_ref[...]
    for g, sg in enumerate(s):
        p = jnp.exp(sg - m_new)
        l_new = l_new + jnp.sum(p, axis=-1, keepdims=True)
        acc = acc + _dot3(p, v_refs[g][0, 0])
    l_ref[...] = l_new
    acc_ref[...] = acc
    m_ref[...] = m_new

    @pl.when(jg == pl.num_programs(1) - 1)
    def _():
        o_ref[0] = (acc_ref[...] / l_ref[...]).astype(o_ref.dtype)


def _fox_sample(q3, kn3, vn3, bias, cache_k4, cache_v4, pt_flat, layer, n_pages):
    db, heads, _ = q3.shape
    rows = cache_k4.shape[2]
    g = _pick(n_pages, (8, 4, 2, 1))
    vec = pl.BlockSpec((1, heads, HEAD_DIM), lambda b, jg, pt: (b, 0, 0))

    def page_spec(i):
        return pl.BlockSpec((1, 1, rows, HEAD_DIM),
                            lambda b, jg, pt: (layer, pt[b * n_pages + jg * g + i], 0, 0))

    grid_spec = pltpu.PrefetchScalarGridSpec(
        num_scalar_prefetch=1,
        grid=(db, n_pages // g),
        in_specs=[vec, vec, vec, pl.BlockSpec((1, n_pages, rows), lambda b, jg, pt: (b, 0, 0))]
                 + [page_spec(i) for i in range(g)] + [page_spec(i) for i in range(g)],
        out_specs=vec,
        scratch_shapes=[pltpu.VMEM((heads, 1), F32), pltpu.VMEM((heads, 1), F32),
                        pltpu.VMEM((heads, HEAD_DIM), F32)],
    )
    return pl.pallas_call(
        functools.partial(_fox_sample_kernel, heads=heads, pages_per_step=g),
        grid_spec=grid_spec,
        out_shape=jax.ShapeDtypeStruct((db, heads, HEAD_DIM), F32),
        compiler_params=_params(("parallel", "arbitrary"), 48),
        name="fox_sample",
    )(pt_flat, q3, kn3, vn3, bias, *([cache_k4] * g), *([cache_v4] * g))


def _route(logits, n_groups, per_group):
    tm = logits.shape[0]
    lane = lax.broadcasted_iota(jnp.int32, (tm, LANES), 1).astype(F32)
    big = float(4 * LANES)
    first_lane_of = lambda hit: jnp.min(jnp.where(hit, lane, big), axis=-1, keepdims=True)
    gl = jnp.where(lane < n_groups, logits, NEG_INF)
    gexp = jnp.exp(gl - jnp.max(gl, axis=-1, keepdims=True))
    p_grp = gexp / jnp.sum(gexp, axis=-1, keepdims=True)
    g_val = jnp.max(p_grp, axis=-1, keepdims=True)
    g_idx = first_lane_of(p_grp == g_val)
    lo = n_groups + g_idx * per_group
    in_grp = (lane >= lo) & (lane < lo + per_group)
    el = jnp.where(in_grp, logits, NEG_INF)
    eexp = jnp.exp(el - jnp.max(el, axis=-1, keepdims=True))
    p_exp = eexp / jnp.sum(eexp, axis=-1, keepdims=True)
    pe = jnp.where(in_grp, p_exp, -1.0)
    v1 = jnp.max(pe, axis=-1, keepdims=True)
    i1 = first_lane_of(pe == v1)
    pe2 = jnp.where(lane == i1, -1.0, pe)
    v2 = jnp.max(pe2, axis=-1, keepdims=True)
    i2 = first_lane_of(pe2 == v2)
    den = v1 + v2
    return jnp.where(lane == 0, i1 - n_groups,
                     jnp.where(lane == 1, i2 - n_groups,
                               jnp.where(lane == 2, g_val * v1 / den,
                                         jnp.where(lane == 3, g_val * v2 / den, 0.0))))


def _store_row_slabs(x, slab_ref):
    tm, n = x.shape[0], x.shape[1] // LANES
    for s in range(n):
        slab_ref[pl.ds(s, tm, stride=n), :] = x[:, s * LANES:(s + 1) * LANES]


def _load_row_slabs(slab_ref, tm, n, pitch):
    return jnp.concatenate([slab_ref[pl.ds(s, tm, stride=pitch), :] for s in range(n)], axis=1)


def _gather_pitch(n):
    return n + 8


def _residual_norm_route(x, y, nw, wr, br, n_groups, per_group):
    x1 = x + y
    ms = jnp.mean(x1 * x1, axis=-1, keepdims=True)
    xn = x1 * lax.rsqrt(ms + EPS) * nw
    return x1, xn, _route(_dot3(xn, wr) + br, n_groups, per_group)


def _outproj_kernel(or_ref, of_ref, x_ref, wo_ref, nw_ref, wr_ref, br_ref,
                    x1_ref, xg_ref, route_ref, *, w_ret, n_groups, per_group):
    y = jnp.dot(or_ref[...], wo_ref[:w_ret, :], preferred_element_type=F32)
    y = y + jnp.dot(of_ref[...], wo_ref[w_ret:, :], preferred_element_type=F32)
    x1_ref[...], xn, route_ref[...] = _residual_norm_route(
        x_ref[...], y, nw_ref[...], wr_ref[...], br_ref[...], n_groups, per_group)
    _store_row_slabs(xn, xg_ref)


def _outproj(o_r, o_f, x, w_out_b, nw, w_r, b_r, n_groups, per_group):
    t, d = x.shape
    w_ret = o_r.shape[1]
    w_fox = o_f.shape[1]
    tm = _pick(t, (320, 256, 128, 64, 8))
    row = lambda i: (i, 0)
    fixed = lambda i: (0, 0)
    return pl.pallas_call(
        functools.partial(_outproj_kernel, w_ret=w_ret, n_groups=n_groups, per_group=per_group),
        grid=(t // tm,),
        in_specs=[pl.BlockSpec((tm, w_ret), row), pl.BlockSpec((tm, w_fox), row), pl.BlockSpec((tm, d), row),
                  pl.BlockSpec((w_ret + w_fox, d), fixed), pl.BlockSpec((1, d), fixed),
                  pl.BlockSpec((d, LANES), fixed), pl.BlockSpec((1, LANES), fixed)],
        out_specs=[pl.BlockSpec((tm, d), row), pl.BlockSpec((tm * (d // LANES), LANES), row),
                   pl.BlockSpec((tm, LANES), row)],
        out_shape=[jax.ShapeDtypeStruct((t, d), F32), jax.ShapeDtypeStruct((t * (d // LANES), LANES), F32),
                   jax.ShapeDtypeStruct((t, LANES), F32)],
        compiler_params=_params(("parallel",), 56),
        name="outproj_router",
    )(o_r, o_f, x, w_out_b, nw.reshape(1, d), w_r, b_r)


def _rms(x, nw):
    ms = jnp.mean(x * x, axis=-1, keepdims=True)
    return x * lax.rsqrt(ms + EPS) * nw


def _sample_proj_kernel(x_ref, nw_ref, w_ref, o_ref):
    o_ref[...] = _dot3(_rms(x_ref[...], nw_ref[...]), w_ref[0], _NT)


def _sample_logf_kernel(x_ref, nw_ref, w_ref, b_ref, o_ref):
    z = _dot3(_rms(x_ref[...], nw_ref[...]), w_ref[0], _NT) + b_ref[...]
    o_ref[...] = _log_sigmoid(z)


def _sample_proj(x, nw, w_t, layer, s_len, db, n_cols):
    d = x.shape[1]
    tn = _pick(n_cols, (512, 256, 128))
    return pl.pallas_call(
        _sample_proj_kernel,
        grid=(n_cols // tn,),
        in_specs=[pl.BlockSpec((db, d), lambda j: (s_len // db, 0)),
                  pl.BlockSpec((1, d), lambda j: (0, 0)),
                  pl.BlockSpec((1, tn, d), lambda j: (layer, j, 0))],
        out_specs=pl.BlockSpec((db, tn), lambda j: (0, j)),
        out_shape=jax.ShapeDtypeStruct((db, n_cols), F32),
        compiler_params=_params(("parallel",), 48),
        name="sample_proj",
    )(x, nw.reshape(1, d), w_t)


def _sample_logf(x, nw, w_t, layer, row0, bias, s_len, db):
    d = x.shape[1]
    n_out = bias.shape[1]
    return pl.pallas_call(
        _sample_logf_kernel,
        grid=(1,),
        in_specs=[pl.BlockSpec((db, d), lambda j: (s_len // db, 0)),
                  pl.BlockSpec((1, d), lambda j: (0, 0)),
                  pl.BlockSpec((1, n_out, d), lambda j: (layer, row0 // n_out, 0)),
                  pl.BlockSpec((1, n_out), lambda j: (0, 0))],
        out_specs=pl.BlockSpec((db, n_out), lambda j: (0, 0)),
        out_shape=jax.ShapeDtypeStruct((db, n_out), F32),
        compiler_params=_params(("arbitrary",), 40),
        name="sample_logf",
    )(x, nw.reshape(1, d), w_t, bias)


def _sample_outproj_kernel(cat_ref, w_ref, y_ref):
    y_ref[...] = _dot3(cat_ref[...], w_ref[0])


def _sample_outproj(cat, w_out, layer):
    db, width = cat.shape
    d = w_out.shape[2]
    tn = _pick(d, (512, 256, 128))
    return pl.pallas_call(
        _sample_outproj_kernel,
        grid=(d // tn,),
        in_specs=[pl.BlockSpec((db, width), lambda j: (0, 0)),
                  pl.BlockSpec((1, width, tn), lambda j: (layer, 0, j))],
        out_specs=pl.BlockSpec((db, tn), lambda j: (0, j)),
        out_shape=jax.ShapeDtypeStruct((db, d), F32),
        compiler_params=_params(("parallel",), 48),
        name="sample_outproj",
    )(cat, w_out)


def _sample_router_kernel(y_ref, x_ref, nw_ref, wr_ref, br_ref, x1_in, xg_in, route_in,
                          x1_ref, xg_ref, route_ref, *, n_groups, per_group):
    del x1_in, xg_in, route_in
    x1_ref[...], xn, route_ref[...] = _residual_norm_route(
        x_ref[...], y_ref[...], nw_ref[...], wr_ref[...], br_ref[...], n_groups, per_group)
    _store_row_slabs(xn, xg_ref)


def _sample_router(y, x, nw, w_r, b_r, x1, xg, route, s_len, n_groups, per_group):
    db, d = y.shape
    blk = s_len // db
    fixed = lambda i: (0, 0)
    rows = lambda i: (blk, 0)
    hbm = pl.BlockSpec(memory_space=pl.ANY)
    return pl.pallas_call(
        functools.partial(_sample_router_kernel, n_groups=n_groups, per_group=per_group),
        grid=(1,),
        in_specs=[pl.BlockSpec((db, d), fixed), pl.BlockSpec((db, d), rows), pl.BlockSpec((1, d), fixed),
                  pl.BlockSpec((d, LANES), fixed), pl.BlockSpec((1, LANES), fixed), hbm, hbm, hbm],
        out_specs=[pl.BlockSpec((db, d), rows), pl.BlockSpec((db * (d // LANES), LANES), rows),
                   pl.BlockSpec((db, LANES), rows)],
        out_shape=[jax.ShapeDtypeStruct(x1.shape, F32), jax.ShapeDtypeStruct(xg.shape, F32),
                   jax.ShapeDtypeStruct(route.shape, F32)],
        input_output_aliases={5: 0, 6: 1, 7: 2},
        compiler_params=_params(("arbitrary",), 40),
        name="sample_router",
    )(y, x, nw.reshape(1, d), w_r, b_r, x1, xg, route)


def _start_row_gather(src_of, hbm, buf, sem, rows, n):
    pitch = _gather_pitch(n)

    def pair(h, carry):
        for u in range(2):
            r = 2 * h + u
            src = pl.multiple_of(src_of(r) * n, n)
            dst = pl.multiple_of(r * pitch, 8)
            pltpu.make_async_copy(hbm.at[pl.ds(src, n)], buf.at[pl.ds(dst, n)], sem).start(priority=u)
        return carry

    lax.fori_loop(0, rows // 2, pair, 0)


def _wait_row_gather(hbm, buf, sem, rows, n):
    pltpu.make_async_copy(hbm.at[pl.ds(0, rows * n)], buf.at[pl.ds(0, rows * n)], sem).wait()


def _expert_kernel(te_ref, nu_ref, src_ref, x_hbm, w1_ref, w3_ref, w2_ref, y_ref, xbuf, sem):
    del te_ref
    i = pl.program_id(0)
    n_used = nu_ref[0]
    n = w1_ref.shape[2] // LANES
    tm = y_ref.shape[0] // n

    def fetch(tile, slot):
        _start_row_gather(lambda r: src_ref[tile * tm + r], x_hbm, xbuf.at[slot], sem.at[slot], tm, n)

    @pl.when(i == 0)
    def _():
        fetch(0, 0)

    @pl.when(i + 1 < n_used)
    def _():
        fetch(i + 1, (i + 1) % 2)

    @pl.when(i < n_used)
    def _():
        slot = i % 2
        _wait_row_gather(x_hbm, xbuf.at[slot], sem.at[slot], tm, n)
        x = _load_row_slabs(xbuf.at[slot], tm, n, _gather_pitch(n)).astype(BF16)
        a = jnp.dot(x, w1_ref[0, 0], preferred_element_type=F32)
        b = jnp.dot(x, w3_ref[0, 0], preferred_element_type=F32)
        hmid = (_silu(a) * b).astype(BF16)
        _store_row_slabs(jnp.dot(hmid, w2_ref[0, 0], preferred_element_type=F32), y_ref)

    @pl.when(i >= n_used)
    def _():
        y_ref[...] = jnp.zeros_like(y_ref)


def _experts(xg, src, w1b, w3b, w2b, layer, tile_expert, n_used, tm, n_tiles):
    d, de = w1b.shape[2], w1b.shape[3]
    n = d // LANES

    def wsel(i, te, nu, s):
        return (layer, te[i], 0, 0)

    grid_spec = pltpu.PrefetchScalarGridSpec(
        num_scalar_prefetch=3,
        grid=(n_tiles,),
        in_specs=[pl.BlockSpec(memory_space=pl.ANY), pl.BlockSpec((1, 1, d, de), wsel),
                  pl.BlockSpec((1, 1, d, de), wsel), pl.BlockSpec((1, 1, de, d), wsel)],
        out_specs=pl.BlockSpec((tm * n, LANES), lambda i, te, nu, s: (i, 0)),
        scratch_shapes=[pltpu.VMEM((2, tm * _gather_pitch(n), LANES), F32), pltpu.SemaphoreType.DMA((2,))],
    )
    return pl.pallas_call(
        _expert_kernel,
        grid_spec=grid_spec,
        out_shape=jax.ShapeDtypeStruct((n_tiles * tm * n, LANES), F32),
        compiler_params=_params(("arbitrary",), 56),
        name="moe_experts",
    )(tile_expert, n_used, src, xg, w1b, w3b, w2b)


def _combine_kernel(pos_ref, x_ref, route_ref, nw_ref, y_hbm, *refs, rows, split_at):
    o_refs, (ybuf, sem) = refs[:-2], refs[-2:]
    i = pl.program_id(0)
    n = x_ref.shape[1] // LANES
    pitch = _gather_pitch(n)

    def fetch(tile, slot):
        for k in range(TOP_K_IN_GROUP):
            _start_row_gather(lambda r: pos_ref[TOP_K_IN_GROUP * (tile * rows + r) + k], y_hbm,
                              ybuf.at[slot, k], sem.at[slot, k], rows, n)

    @pl.when(i == 0)
    def _():
        fetch(0, 0)

    @pl.when(i + 1 < pl.num_programs(0))
    def _():
        fetch(i + 1, (i + 1) % 2)

    slot = i % 2
    for k in range(TOP_K_IN_GROUP):
        _wait_row_gather(y_hbm, ybuf.at[slot, k], sem.at[slot, k], rows, n)
    route = route_ref[...]
    wt0 = jnp.broadcast_to(route[:, 2:3], (rows, LANES))
    wt1 = jnp.broadcast_to(route[:, 3:4], (rows, LANES))
    moe = jnp.concatenate([wt0 * ybuf[slot, 0, pl.ds(s, rows, stride=pitch), :]
                           + wt1 * ybuf[slot, 1, pl.ds(s, rows, stride=pitch), :] for s in range(n)], axis=1)
    x2 = x_ref[...] + moe
    ms = jnp.mean(x2 * x2, axis=-1, keepdims=True)
    xn = x2 * lax.rsqrt(ms + EPS) * nw_ref[...]
    if split_at is None:
        x2_ref, xn_ref = o_refs
        x2_ref[...] = x2
        xn_ref[...] = xn.astype(xn_ref.dtype)
    else:
        yp_ref, ys_ref = o_refs

        @pl.when(i < split_at)
        def _():
            yp_ref[...] = xn

        @pl.when(i >= split_at)
        def _():
            ys_ref[...] = xn


def _combine(x1, route, y_sorted, pos_flat, nw, split_rows=None):
    t, d = x1.shape
    rows = _pick(t, (128, 64, 8))
    row = lambda i, p: (i, 0)
    if split_rows is None:
        split_at = None
        out_specs = [pl.BlockSpec((rows, d), row), pl.BlockSpec((rows, d), row)]
        out_shape = [jax.ShapeDtypeStruct((t, d), F32), jax.ShapeDtypeStruct((t, d), BF16)]
    else:
        split_at = split_rows // rows
        out_specs = [pl.BlockSpec((rows, d), lambda i, p: (jnp.minimum(i, split_at - 1), 0)),
                     pl.BlockSpec((rows, d), lambda i, p: (jnp.maximum(i - split_at, 0), 0))]
        out_shape = [jax.ShapeDtypeStruct((split_rows, d), F32), jax.ShapeDtypeStruct((t - split_rows, d), F32)]
    grid_spec = pltpu.PrefetchScalarGridSpec(
        num_scalar_prefetch=1,
        grid=(t // rows,),
        in_specs=[pl.BlockSpec((rows, d), row), pl.BlockSpec((rows, LANES), row),
                  pl.BlockSpec((1, d), lambda i, p: (0, 0)), pl.BlockSpec(memory_space=pl.ANY)],
        out_specs=out_specs,
        scratch_shapes=[pltpu.VMEM((2, TOP_K_IN_GROUP, rows * _gather_pitch(d // LANES), LANES), F32),
                        pltpu.SemaphoreType.DMA((2, TOP_K_IN_GROUP))],
    )
    return pl.pallas_call(
        functools.partial(_combine_kernel, rows=rows, split_at=split_at),
        grid_spec=grid_spec,
        out_shape=out_shape,
        compiler_params=_params(("arbitrary",), 40),
        name="moe_combine",
    )(pos_flat, x1, route, nw.reshape(1, d), y_sorted)


def _routing_tables(e_id, n_experts, tm, n_tiles):
    flat = e_id.reshape(-1)
    onehot = (flat[:, None] == jnp.arange(n_experts, dtype=jnp.int32)[None, :]).astype(jnp.int32)
    csum = jnp.cumsum(onehot, axis=0)
    rank = jnp.sum((csum - onehot) * onehot, axis=1)
    counts = csum[-1]
    padded = ((counts + tm - 1) // tm) * tm
    ends = jnp.cumsum(padded)
    starts = ends - padded
    pos = (jnp.sum(onehot * starts[None, :], axis=1) + rank).astype(jnp.int32)
    token = jnp.arange(flat.shape[0], dtype=jnp.int32) // TOP_K_IN_GROUP
    src = jnp.zeros((n_tiles * tm,), jnp.int32).at[pos].set(token)
    tile_start = jnp.arange(n_tiles, dtype=jnp.int32) * tm
    n_used = (ends[-1] // tm).astype(jnp.int32)
    tile_expert = jnp.sum((tile_start[:, None] >= ends[None, :]).astype(jnp.int32), axis=1)
    last_expert = jnp.sum((jnp.maximum(ends[-1] - 1, 0) >= ends).astype(jnp.int32))
    tile_expert = jnp.minimum(tile_expert, last_expert).astype(jnp.int32)
    return src, pos, tile_expert, n_used.reshape(1)


def kernel(x_prompt, x_sample, cache_k, cache_v, cache_logf, state_ret, page_table, norm_mix_w, w_in, b_fgt,
           ret_gn_w, w_out, norm_ffn_w, w_group, b_group, w_expert, b_expert, w1, w3, w2, norm_final_w):
    assert x_prompt.shape[0] == 1 and x_sample.shape[1] == 1
    s_len, d = x_prompt.shape[1], x_prompt.shape[2]
    db = x_sample.shape[0]
    depth = w_in.shape[0]
    n_pool, page, h_fox = cache_k.shape[1], cache_k.shape[2], cache_k.shape[3]
    h_ret = state_ret.shape[2]
    w_ret, w_fox = h_ret * HEAD_DIM, h_fox * HEAD_DIM
    n_pages = page_table.shape[1]
    past_len = n_pages * page
    n_groups, n_experts = w_group.shape[-1], w_expert.shape[-1]
    per_group = n_experts // n_groups
    t = s_len + db
    chunk = 128
    tq_f = _pick(s_len, (512, 256, 128))
    tm_e = 256
    n_tiles = (TOP_K_IN_GROUP * t + n_experts * (tm_e - 1) + tm_e - 1) // tm_e

    inv_freq = ROPE_BASE ** (-jnp.arange(0, HEAD_DIM, 2, dtype=F32) / HEAD_DIM)
    posv = jnp.concatenate([jnp.arange(s_len), jnp.full((db,), past_len)]).astype(F32)
    ang = posv[:, None] * inv_freq[None, :]
    cos2 = jnp.concatenate([jnp.cos(ang), jnp.cos(ang)], axis=-1)
    sin2 = jnp.concatenate([-jnp.sin(ang), jnp.sin(ang)], axis=-1)

    cache_k4 = cache_k.reshape(depth, n_pool, page * h_fox, HEAD_DIM)
    cache_v4 = cache_v.reshape(depth, n_pool, page * h_fox, HEAD_DIM)
    pt_flat = page_table.reshape(-1)
    assert w_ret == w_fox
    w_t = jnp.swapaxes(w_in, 1, 2)

    w1b, w3b, w2b = w1.astype(BF16), w3.astype(BF16), w2.astype(BF16)

    x = jnp.concatenate([x_prompt[0], x_sample[:, 0]], axis=0)
    xn = _rmsnorm(x, norm_mix_w[0], BF16)

    k_p, v_p, lf_p, st_p, k_s, v_s, lf_s, st_s = [], [], [], [], [], [], [], []
    for l in range(depth):
        qk = _proj_rope(xn, w_t, l, cos2, sin2, w_ret, s_len)
        vg, = _proj(xn, w_t, l, 2, 2, (F32,), w_ret, s_len, name="proj_vg")
        c0 = 4 * w_ret
        qf, = _proj(xn, w_t, l, 4, 1, (BF16,), w_ret, s_len, scale=HEAD_DIM ** -0.5 * LOG2E, name="proj_qf")
        kf, kb = _proj(xn, w_t, l, 5, 1, (F32, BF16), w_ret, s_len, name="proj_k")
        vf, vb = _proj(xn, w_t, l, 6, 1, (F32, BF16), w_ret, s_len, name="proj_v")
        b_f = b_fgt[l].reshape(1, h_fox)
        logf = _proj_logf(xn, w_t, l, c0 + 3 * w_fox, b_f, s_len)
        o_r_p, s_fin = _retention_prompt(qk, vg, ret_gn_w[l], s_len, w_ret, chunk)
        c2 = (jnp.cumsum(logf, axis=0) * LOG2E).T
        o_f_p = _fox_prompt(qf, kb, vb, c2[:, :, None], c2.reshape(h_fox, s_len // tq_f, 1, tq_f))

        nw_mix = norm_mix_w[l]
        p_s = _sample_proj(x, nw_mix, w_t, l, s_len, db, c0 + 3 * w_fox)
        logf_s = _sample_logf(x, nw_mix, w_t, l, c0 + 3 * w_fox, b_f, s_len, db)
        heads3 = lambda w0, h: p_s[:, w0:w0 + h * HEAD_DIM].reshape(db, h, HEAD_DIM)
        o_r_s, s_new = _retention_sample(heads3(0, h_ret), heads3(w_ret, h_ret), heads3(2 * w_ret, h_ret),
                                         heads3(3 * w_ret, h_ret), cos2[s_len:s_len + 1], sin2[s_len:s_len + 1],
                                         ret_gn_w[l], state_ret, l)
        lf_past = cache_logf[l][page_table].reshape(db, past_len, h_fox)
        suffix = jnp.flip(jnp.cumsum(jnp.flip(lf_past, axis=1), axis=1), axis=1) - lf_past
        bias = (suffix + logf_s[:, None, :]).reshape(db, n_pages, page * h_fox)
        o_f_s = _fox_sample(heads3(c0, h_fox), heads3(c0 + w_fox, h_fox), heads3(c0 + 2 * w_fox, h_fox), bias,
                            cache_k4, cache_v4, pt_flat, l, n_pages)
        cat_s = jnp.concatenate([o_r_s.reshape(db, w_ret), o_f_s.reshape(db, w_fox)], axis=1)

        o_r = jnp.concatenate([o_r_p, cat_s[:, :w_ret].astype(BF16)], axis=0)
        o_f = jnp.concatenate([o_f_p, cat_s[:, w_ret:].astype(BF16)], axis=0)

        w_r = jnp.zeros((d, LANES), F32).at[:, :n_groups].set(w_group[l]).at[:, n_groups:n_groups + n_experts].set(w_expert[l])
        b_r = jnp.zeros((1, LANES), F32).at[0, :n_groups].set(b_group[l]).at[0, n_groups:n_groups + n_experts].set(b_expert[l])
        x1, xg, route = _outproj(o_r, o_f, x, w_out[l].astype(BF16), norm_ffn_w[l], w_r, b_r, n_groups, per_group)
        y_s = _sample_outproj(cat_s, w_out, l)
        x1, xg, route = _sample_router(y_s, x, norm_ffn_w[l], w_r, b_r, x1, xg, route, s_len, n_groups, per_group)

        e_id = route[:, :TOP_K_IN_GROUP].astype(jnp.int32)
        src, pos, tile_expert, n_used = _routing_tables(e_id, n_experts, tm_e, n_tiles)
        y_sorted = _experts(xg, src, w1b, w3b, w2b, l, tile_expert, n_used, tm_e, n_tiles)
        if l < depth - 1:
            x, xn = _combine(x1, route, y_sorted, pos, norm_mix_w[l + 1])
        else:
            y_prompt, y_sample = _combine(x1, route, y_sorted, pos, norm_final_w, split_rows=s_len)

        k_p.append(kf.reshape(1, s_len, h_fox, HEAD_DIM))
        v_p.append(vf.reshape(1, s_len, h_fox, HEAD_DIM))
        lf_p.append(logf.reshape(1, s_len, h_fox))
        st_p.append(s_fin.reshape(1, h_ret, HEAD_DIM, HEAD_DIM))
        k_s.append(heads3(c0 + w_fox, h_fox).reshape(db, 1, h_fox, HEAD_DIM))
        v_s.append(heads3(c0 + 2 * w_fox, h_fox).reshape(db, 1, h_fox, HEAD_DIM))
        lf_s.append(logf_s.reshape(db, 1, h_fox))
        st_s.append(s_new)

    return (y_prompt.reshape(1, s_len, d), y_sample.reshape(db, 1, d), jnp.stack(k_p), jnp.stack(v_p), jnp.stack(lf_p), jnp.stack(st_p),
            jnp.stack(k_s), jnp.stack(v_s), jnp.stack(lf_s), jnp.stack(st_s))
```

```python
import functools

import jax
import jax.numpy as jnp
from jax import lax
from jax.experimental import pallas as pl
from jax.experimental.pallas import tpu as pltpu

F32 = jnp.float32
BF16 = jnp.bfloat16

HEAD_DIM = 128
EPS = 1e-6
ROPE_BASE = 10000.0
TOP_K_IN_GROUP = 2
LANES = 128
MIB = 1 << 20
NEG_INF = float("-inf")
LOG2E = 1.4426950408889634


def _pick(n, cands):
    for c in cands:
        if n % c == 0:
            return c
    raise ValueError(f"no tile for {n} in {cands}")


def _params(sem, vmem_mib):
    return pltpu.CompilerParams(dimension_semantics=sem, vmem_limit_bytes=vmem_mib * MIB)


def _silu(x):
    return x * (1.0 / (1.0 + jnp.exp(-x)))


def _split(a):
    hi = a.astype(BF16)
    return hi, (a - hi.astype(F32)).astype(BF16)


_NN = (((1,), (0,)), ((), ()))
_NT = (((1,), (1,)), ((), ()))


def _dot3(a, b, dims=_NN):
    ah, al = _split(a)
    bh, bl = _split(b)
    dot = lambda u, v: lax.dot_general(u, v, dims, preferred_element_type=F32)
    m = a.shape[0]
    top = dot(jnp.concatenate([ah, al], axis=0), bh)
    return top[:m] + top[m:] + dot(ah, bl)


def _rmsnorm_kernel(x_ref, w_ref, o_ref):
    x = x_ref[...]
    ms = jnp.mean(x * x, axis=-1, keepdims=True)
    o_ref[...] = (x * lax.rsqrt(ms + EPS) * w_ref[...]).astype(o_ref.dtype)


def _rmsnorm(x, w, out_dtype):
    t, d = x.shape
    tm = _pick(t, (640, 320, 256, 128, 64, 8))
    return pl.pallas_call(
        _rmsnorm_kernel,
        grid=(t // tm,),
        in_specs=[pl.BlockSpec((tm, d), lambda i: (i, 0)), pl.BlockSpec((1, d), lambda i: (0, 0))],
        out_specs=pl.BlockSpec((tm, d), lambda i: (i, 0)),
        out_shape=jax.ShapeDtypeStruct((t, d), out_dtype),
        compiler_params=_params(("parallel",), 40),
        name="rmsnorm",
    )(x, w.reshape(1, d))


def _load_weight_block(w_ref, wb_ref):
    step = 2 * LANES

    @pl.when(pl.program_id(1) == 0)
    def _():
        for c in range(w_ref.shape[1] // step):
            wb_ref[:, c * step:(c + 1) * step] = w_ref[0, c * step:(c + 1) * step, :].T.astype(BF16)


def _proj_kernel(x_ref, w_ref, *refs, scale):
    o_refs, wb_ref = refs[:-1], refs[-1]
    _load_weight_block(w_ref, wb_ref)
    acc = jnp.dot(x_ref[...], wb_ref[...], preferred_element_type=F32)
    if scale != 1.0:
        acc = acc * scale
    for o_ref in o_refs:
        o_ref[...] = acc.astype(o_ref.dtype)


def _proj_rope_kernel(x_ref, w_ref, cos_ref, sin_ref, o_ref, wb_ref, *, heads, kscale):
    j = pl.program_id(0)
    _load_weight_block(w_ref, wb_ref)
    acc = jnp.dot(x_ref[...], wb_ref[...], preferred_element_type=F32)
    cos = cos_ref[...]
    sin = sin_ref[...]
    scale = jnp.where(j == 1, kscale, 1.0).astype(F32)
    for h in range(heads):
        sl = slice(h * HEAD_DIM, (h + 1) * HEAD_DIM)
        xh = acc[:, sl]
        r = xh * cos + pltpu.roll(xh, HEAD_DIM // 2, axis=1) * sin
        o_ref[:, sl] = (r * scale).astype(o_ref.dtype)


def _log_sigmoid(z):
    return jnp.minimum(z, 0.0) - jnp.log1p(jnp.exp(-jnp.abs(z)))


def _proj_logf_kernel(x_ref, w_ref, b_ref, o_ref):
    z = lax.dot_general(x_ref[...], w_ref[0].astype(BF16), _NT, preferred_element_type=F32)
    o_ref[...] = _log_sigmoid(z + b_ref[...])


def _proj(xn, w_t, layer, blk0, nblk, out_dtypes, tn, t, scale=1.0, name="proj"):
    d = xn.shape[1]
    tm = _pick(t, (512, 256, 128, 64, 8))
    outs = pl.pallas_call(
        functools.partial(_proj_kernel, scale=scale),
        grid=(nblk, t // tm),
        in_specs=[pl.BlockSpec((tm, d), lambda j, i: (i, 0)),
                  pl.BlockSpec((1, tn, d), lambda j, i: (layer, blk0 + j, 0))],
        out_specs=[pl.BlockSpec((tm, tn), lambda j, i: (i, j)) for _ in out_dtypes],
        out_shape=[jax.ShapeDtypeStruct((t, nblk * tn), dt) for dt in out_dtypes],
        scratch_shapes=[pltpu.VMEM((d, tn), BF16)],
        compiler_params=_params(("arbitrary", "arbitrary"), 56),
        name=name,
    )(xn, w_t)
    return outs


def _proj_rope(xn, w_t, layer, cos2, sin2, w_ret, t):
    d = xn.shape[1]
    tm = _pick(t, (512, 256, 128, 64, 8))
    return pl.pallas_call(
        functools.partial(_proj_rope_kernel, heads=w_ret // HEAD_DIM, kscale=HEAD_DIM ** -0.5),
        grid=(2, t // tm),
        in_specs=[pl.BlockSpec((tm, d), lambda j, i: (i, 0)),
                  pl.BlockSpec((1, w_ret, d), lambda j, i: (layer, j, 0)),
                  pl.BlockSpec((tm, HEAD_DIM), lambda j, i: (i, 0)),
                  pl.BlockSpec((tm, HEAD_DIM), lambda j, i: (i, 0))],
        out_specs=pl.BlockSpec((tm, w_ret), lambda j, i: (i, j)),
        out_shape=jax.ShapeDtypeStruct((t, 2 * w_ret), BF16),
        scratch_shapes=[pltpu.VMEM((d, w_ret), BF16)],
        compiler_params=_params(("arbitrary", "arbitrary"), 56),
        name="proj_rope",
    )(xn, w_t, cos2, sin2)


def _proj_logf(xn, w_t, layer, row0, bias, t):
    d = xn.shape[1]
    n_out = bias.shape[1]
    tm = _pick(t, (512, 256, 128, 64, 8))
    return pl.pallas_call(
        _proj_logf_kernel,
        grid=(t // tm,),
        in_specs=[pl.BlockSpec((tm, d), lambda i: (i, 0)),
                  pl.BlockSpec((1, n_out, d), lambda i: (layer, row0 // n_out, 0)),
                  pl.BlockSpec((1, n_out), lambda i: (0, 0))],
        out_specs=pl.BlockSpec((tm, n_out), lambda i: (i, 0)),
        out_shape=jax.ShapeDtypeStruct((t, n_out), F32),
        compiler_params=_params(("parallel",), 40),
        name="proj_logf",
    )(xn, w_t, bias)


def _groupnorm_gate(o, g, gnw):
    mu = jnp.mean(o, axis=-1, keepdims=True)
    oc = o - mu
    var = jnp.mean(oc * oc, axis=-1, keepdims=True)
    return _silu(g) * (oc * lax.rsqrt(var + EPS) * gnw)


def _ret_prompt_kernel(q_ref, k_ref, v_ref, g_ref, gnw_ref, decay_ref, xi_ref, zeta_ref, gc_ref,
                       o_ref, state_ref, *, heads):
    c = pl.program_id(0)

    @pl.when(c == 0)
    def _():
        state_ref[...] = jnp.zeros_like(state_ref)

    for h in range(heads):
        sl = slice(h * HEAD_DIM, (h + 1) * HEAD_DIM)
        q = q_ref[:, sl]
        k = k_ref[:, sl]
        v = v_ref[:, sl]
        st = state_ref[h]
        s = lax.dot_general(q, k, (((1,), (1,)), ((), ())), preferred_element_type=F32) * decay_ref[h]
        o = jnp.dot(s.astype(BF16), v.astype(BF16), preferred_element_type=F32)
        o = o + jnp.dot(q, st.astype(BF16), preferred_element_type=F32) * xi_ref[h]
        vz = (v * zeta_ref[h]).astype(BF16)
        kv = lax.dot_general(k, vz, (((0,), (0,)), ((), ())), preferred_element_type=F32)
        state_ref[h] = gc_ref[h] * st + kv
        o_ref[:, sl] = _groupnorm_gate(o, g_ref[:, sl], gnw_ref[:, sl]).astype(o_ref.dtype)


def _retention_tables(heads, chunk):
    lg = jnp.log1p(-jnp.exp2(-5.0 - jnp.arange(heads, dtype=F32)))
    idx = jnp.arange(chunk, dtype=F32)
    diff = idx[:, None] - idx[None, :]
    decay = jnp.where(diff >= 0, jnp.exp(lg[:, None, None] * jnp.maximum(diff, 0.0)), 0.0)
    xi = jnp.exp(lg[:, None] * (idx[None, :] + 1.0))
    zeta = jnp.exp(lg[:, None] * (chunk - 1.0 - idx[None, :]))
    ones = jnp.ones((1, 1, HEAD_DIM), F32)
    return decay, xi[:, :, None] * ones, zeta[:, :, None] * ones, jnp.exp(lg * chunk)[:, None, None] * ones


def _retention_prompt(qk, vg, gnw, s_len, w_ret, chunk):
    heads = w_ret // HEAD_DIM
    decay, xi, zeta, gamma_c = _retention_tables(heads, chunk)
    tbl = pl.BlockSpec((heads, chunk, HEAD_DIM), lambda c: (0, 0, 0))
    return pl.pallas_call(
        functools.partial(_ret_prompt_kernel, heads=heads),
        grid=(s_len // chunk,),
        in_specs=[pl.BlockSpec((chunk, w_ret), lambda c: (c, 0)),
                  pl.BlockSpec((chunk, w_ret), lambda c: (c, 1)),
                  pl.BlockSpec((chunk, w_ret), lambda c: (c, 0)),
                  pl.BlockSpec((chunk, w_ret), lambda c: (c, 1)),
                  pl.BlockSpec((1, w_ret), lambda c: (0, 0)),
                  tbl, tbl, tbl, pl.BlockSpec((heads, 1, HEAD_DIM), lambda c: (0, 0, 0))],
        out_specs=[pl.BlockSpec((chunk, w_ret), lambda c: (c, 0)),
                   pl.BlockSpec((heads, HEAD_DIM, HEAD_DIM), lambda c: (0, 0, 0))],
        out_shape=[jax.ShapeDtypeStruct((s_len, w_ret), BF16),
                   jax.ShapeDtypeStruct((heads, HEAD_DIM, HEAD_DIM), F32)],
        compiler_params=_params(("arbitrary",), 40),
        name="retention_prompt",
    )(qk, qk, vg, vg, gnw.reshape(1, w_ret), decay, xi, zeta, gamma_c)


def _ret_sample_kernel(q_ref, k_ref, v_ref, g_ref, cos_ref, sin_ref, gnw_ref, gamma_ref, st_ref, o_ref, ns_ref, *,
                       heads):
    cos = cos_ref[...]
    sin = sin_ref[...]
    rope = lambda a: a * cos + pltpu.roll(a, HEAD_DIM // 2, axis=1) * sin
    q = rope(q_ref[0])
    k = rope(k_ref[0]) * (HEAD_DIM ** -0.5)
    v = v_ref[0]
    qk = jnp.sum(q * k, axis=-1, keepdims=True)
    pad = jnp.zeros((HEAD_DIM - 2 * heads, HEAD_DIM), F32)
    cols = jnp.concatenate([q, k, pad], axis=0).T
    rows = []
    for h in range(heads):
        st = st_ref[0, 0, h]
        qcol = cols[:, h:h + 1]
        kcol = cols[:, heads + h:heads + h + 1]
        vrow = v[h:h + 1, :]
        gamma = gamma_ref[h]
        rows.append(gamma * jnp.sum(st * qcol, axis=0, keepdims=True) + qk[h:h + 1, :] * vrow)
        ns_ref[0, h] = gamma * st + kcol * vrow
    o = jnp.concatenate(rows, axis=0)
    o_ref[0] = _groupnorm_gate(o, g_ref[0], gnw_ref[...]).astype(o_ref.dtype)


def _retention_sample(q3, k3, v3, g3, cos_row, sin_row, gnw, state_ret, layer):
    db, heads, _ = q3.shape
    gamma = _retention_tables(heads, 1)[3]
    vec = pl.BlockSpec((1, heads, HEAD_DIM), lambda b: (b, 0, 0))
    row = pl.BlockSpec((1, HEAD_DIM), lambda b: (0, 0))
    return pl.pallas_call(
        functools.partial(_ret_sample_kernel, heads=heads),
        grid=(db,),
        in_specs=[vec, vec, vec, vec, row, row,
                  pl.BlockSpec((heads, HEAD_DIM), lambda b: (0, 0)),
                  pl.BlockSpec((heads, 1, HEAD_DIM), lambda b: (0, 0, 0)),
                  pl.BlockSpec((1, 1, heads, HEAD_DIM, HEAD_DIM), lambda b: (layer, b, 0, 0, 0))],
        out_specs=[vec, pl.BlockSpec((1, heads, HEAD_DIM, HEAD_DIM), lambda b: (b, 0, 0, 0))],
        out_shape=[jax.ShapeDtypeStruct((db, heads, HEAD_DIM), F32),
                   jax.ShapeDtypeStruct((db, heads, HEAD_DIM, HEAD_DIM), F32)],
        compiler_params=_params(("parallel",), 40),
        name="retention_sample",
    )(q3, k3, v3, g3, cos_row, sin_row, gnw.reshape(heads, HEAD_DIM), gamma, state_ret)


AUG = 2 * HEAD_DIM
V_ROWS = HEAD_DIM + 16


def _split3(c):
    to_bf16 = lambda a: lax.reduce_precision(a, exponent_bits=8, mantissa_bits=7)
    hi = to_bf16(c)
    mid = to_bf16(c - hi)
    lo = to_bf16(c - hi - mid)
    return hi.astype(BF16), mid.astype(BF16), lo.astype(BF16)


def _fox_prompt_t_kernel(qa_ref, ka_ref, v1_ref, o_ref, m_ref, acc_ref, *, tq, col_chunks):
    qi = pl.program_id(1)
    tc = tq // col_chunks
    m_ref[...] = jnp.full_like(m_ref, NEG_INF)
    acc_ref[...] = jnp.zeros_like(acc_ref)

    def weights(kj, masked):
        start = pl.multiple_of(kj * tq, tq)
        ka = ka_ref[pl.ds(start, tq), :]
        ps, alphas = [], []
        for c in range(col_chunks):
            cols = pl.ds(c * tc, tc)
            t = jnp.dot(ka, qa_ref[0, 0, :, cols], preferred_element_type=F32)
            if masked:
                key = lax.broadcasted_iota(jnp.int32, (tq, tc), 0)
                qry = lax.broadcasted_iota(jnp.int32, (tq, tc), 1) + c * tc
                t = jnp.where(key <= qry, t, NEG_INF)
            m_prev = m_ref[:, cols]
            m_new = jnp.maximum(m_prev, jnp.max(t, axis=0, keepdims=True))
            m_ref[:, cols] = m_new
            ps.append(jnp.exp2(t - m_new).astype(BF16))
            alphas.append(jnp.exp2(m_prev - m_new))
        return jnp.concatenate(ps, axis=1), jnp.concatenate(alphas, axis=1)

    def accumulate(kj, p, alpha):
        acc_ref[...] = alpha * acc_ref[...] + jnp.dot(v1_ref[0, kj], p, preferred_element_type=F32)

    def body(kj, carry):
        accumulate(jnp.maximum(kj - 1, 0), *carry)
        return weights(kj, False)

    carry = lax.fori_loop(0, qi, body, (jnp.zeros((tq, tq), BF16), jnp.ones((1, tq), F32)))
    accumulate(jnp.maximum(qi - 1, 0), *carry)
    accumulate(qi, *weights(qi, True))
    o_t = acc_ref[:HEAD_DIM, :] / acc_ref[HEAD_DIM:HEAD_DIM + 1, :]
    o_ref[...] = o_t.T.astype(o_ref.dtype)


def _fox_prompt_t(qa, ka, v1, s_len, w_fox):
    heads, nq, _, tq = qa.shape
    return pl.pallas_call(
        functools.partial(_fox_prompt_t_kernel, tq=tq, col_chunks=2),
        grid=(heads, nq),
        in_specs=[pl.BlockSpec((1, 1, AUG, tq), lambda h, qi: (h, qi, 0, 0)),
                  pl.BlockSpec((s_len, AUG), lambda h, qi: (0, h)),
                  pl.BlockSpec((1, nq, V_ROWS, tq), lambda h, qi: (h, 0, 0, 0))],
        out_specs=pl.BlockSpec((tq, HEAD_DIM), lambda h, qi: (qi, h)),
        out_shape=jax.ShapeDtypeStruct((s_len, w_fox), BF16),
        scratch_shapes=[pltpu.VMEM((1, tq), F32), pltpu.VMEM((V_ROWS, tq), F32)],
        compiler_params=_params(("parallel", "arbitrary"), 40),
        name="fox_prompt",
    )(qa, ka, v1)


def _fox_sample_kernel(pt_ref, q_ref, kn_ref, vn_ref, bias_ref, *refs, heads, pages_per_step):
    del pt_ref
    k_refs = refs[:pages_per_step]
    v_refs = refs[pages_per_step:2 * pages_per_step]
    o_ref, m_ref, l_ref, acc_ref = refs[2 * pages_per_step:]
    jg = pl.program_id(1)
    q = q_ref[0] * (HEAD_DIM ** -0.5)

    @pl.when(jg == 0)
    def _():
        m_ref[...] = jnp.sum(q * kn_ref[0], axis=-1, keepdims=True)
        l_ref[...] = jnp.ones_like(l_ref)
        acc_ref[...] = vn_ref[0]

    rows = k_refs[0].shape[2]
    sub = lax.broadcasted_iota(jnp.int32, (heads, rows), 0)
    lane = lax.broadcasted_iota(jnp.int32, (heads, rows), 1)
    own_head = (lane % heads) == sub
    s = []
    for g in range(pages_per_step):
        sg = _dot3(q, k_refs[g][0, 0], _NT)
        sg = sg + bias_ref[0, pl.ds(jg * pages_per_step + g, 1), :]
        s.append(jnp.where(own_head, sg, NEG_INF))
    m_prev = m_ref[...]
    m_new = m_prev
    for sg in s:
        m_new = jnp.maximum(m_new, jnp.max(sg, axis=-1, keepdims=True))
    alpha = jnp.exp(m_prev - m_new)
    l_new = alpha * l_ref[...]
    acc = alpha * acc_ref[...]
    for g, sg in enumerate(s):
        p = jnp.exp(sg - m_new)
        l_new = l_new + jnp.sum(p, axis=-1, keepdims=True)
        acc = acc + _dot3(p, v_refs[g][0, 0])
    l_ref[...] = l_new
    acc_ref[...] = acc
    m_ref[...] = m_new

    @pl.when(jg == pl.num_programs(1) - 1)
    def _():
        o_ref[0] = (acc_ref[...] / l_ref[...]).astype(o_ref.dtype)


def _fox_sample(q3, kn3, vn3, bias, cache_k4, cache_v4, pt_flat, layer, n_pages):
    db, heads, _ = q3.shape
    rows = cache_k4.shape[2]
    g = _pick(n_pages, (8, 4, 2, 1))
    vec = pl.BlockSpec((1, heads, HEAD_DIM), lambda b, jg, pt: (b, 0, 0))

    def page_spec(i):
        return pl.BlockSpec((1, 1, rows, HEAD_DIM),
                            lambda b, jg, pt: (layer, pt[b * n_pages + jg * g + i], 0, 0))

    grid_spec = pltpu.PrefetchScalarGridSpec(
        num_scalar_prefetch=1,
        grid=(db, n_pages // g),
        in_specs=[vec, vec, vec, pl.BlockSpec((1, n_pages, rows), lambda b, jg, pt: (b, 0, 0))]
                 + [page_spec(i) for i in range(g)] + [page_spec(i) for i in range(g)],
        out_specs=vec,
        scratch_shapes=[pltpu.VMEM((heads, 1), F32), pltpu.VMEM((heads, 1), F32),
                        pltpu.VMEM((heads, HEAD_DIM), F32)],
    )
    return pl.pallas_call(
        functools.partial(_fox_sample_kernel, heads=heads, pages_per_step=g),
        grid_spec=grid_spec,
        out_shape=jax.ShapeDtypeStruct((db, heads, HEAD_DIM), F32),
        compiler_params=_params(("parallel", "arbitrary"), 48),
        name="fox_sample",
    )(pt_flat, q3, kn3, vn3, bias, *([cache_k4] * g), *([cache_v4] * g))


def _route(logits, n_groups, per_group):
    tm = logits.shape[0]
    lane = lax.broadcasted_iota(jnp.int32, (tm, LANES), 1).astype(F32)
    big = float(4 * LANES)
    first_lane_of = lambda hit: jnp.min(jnp.where(hit, lane, big), axis=-1, keepdims=True)
    gl = jnp.where(lane < n_groups, logits, NEG_INF)
    gexp = jnp.exp(gl - jnp.max(gl, axis=-1, keepdims=True))
    p_grp = gexp / jnp.sum(gexp, axis=-1, keepdims=True)
    g_val = jnp.max(p_grp, axis=-1, keepdims=True)
    g_idx = first_lane_of(p_grp == g_val)
    lo = n_groups + g_idx * per_group
    in_grp = (lane >= lo) & (lane < lo + per_group)
    el = jnp.where(in_grp, logits, NEG_INF)
    eexp = jnp.exp(el - jnp.max(el, axis=-1, keepdims=True))
    p_exp = eexp / jnp.sum(eexp, axis=-1, keepdims=True)
    pe = jnp.where(in_grp, p_exp, -1.0)
    v1 = jnp.max(pe, axis=-1, keepdims=True)
    i1 = first_lane_of(pe == v1)
    pe2 = jnp.where(lane == i1, -1.0, pe)
    v2 = jnp.max(pe2, axis=-1, keepdims=True)
    i2 = first_lane_of(pe2 == v2)
    den = v1 + v2
    return jnp.where(lane == 0, i1 - n_groups,
                     jnp.where(lane == 1, i2 - n_groups,
                               jnp.where(lane == 2, g_val * v1 / den,
                                         jnp.where(lane == 3, g_val * v2 / den, 0.0))))


def _store_row_slabs(x, slab_ref):
    tm, n = x.shape[0], x.shape[1] // LANES
    for s in range(n):
        slab_ref[pl.ds(s, tm, stride=n), :] = x[:, s * LANES:(s + 1) * LANES]


def _load_row_slabs(slab_ref, tm, n, pitch):
    return jnp.concatenate([slab_ref[pl.ds(s, tm, stride=pitch), :] for s in range(n)], axis=1)


def _gather_pitch(n):
    return n + 8


def _residual_norm_route(x, y, nw, wr, br, n_groups, per_group):
    x1 = x + y
    ms = jnp.mean(x1 * x1, axis=-1, keepdims=True)
    xn = x1 * lax.rsqrt(ms + EPS) * nw
    return x1, xn, _route(_dot3(xn, wr) + br, n_groups, per_group)


def _outproj_kernel(or_ref, of_ref, x_ref, wo_ref, nw_ref, wr_ref, br_ref,
                    x1_ref, xg_ref, route_ref, *, w_ret, n_groups, per_group):
    y = jnp.dot(or_ref[...], wo_ref[:w_ret, :], preferred_element_type=F32)
    y = y + jnp.dot(of_ref[...], wo_ref[w_ret:, :], preferred_element_type=F32)
    x1_ref[...], xn, route_ref[...] = _residual_norm_route(
        x_ref[...], y, nw_ref[...], wr_ref[...], br_ref[...], n_groups, per_group)
    _store_row_slabs(xn, xg_ref)


def _outproj(o_r, o_f, x, w_out_b, nw, w_r, b_r, n_groups, per_group):
    t, d = x.shape
    w_ret = o_r.shape[1]
    w_fox = o_f.shape[1]
    tm = _pick(t, (320, 256, 128, 64, 8))
    row = lambda i: (i, 0)
    fixed = lambda i: (0, 0)
    return pl.pallas_call(
        functools.partial(_outproj_kernel, w_ret=w_ret, n_groups=n_groups, per_group=per_group),
        grid=(t // tm,),
        in_specs=[pl.BlockSpec((tm, w_ret), row), pl.BlockSpec((tm, w_fox), row), pl.BlockSpec((tm, d), row),
                  pl.BlockSpec((w_ret + w_fox, d), fixed), pl.BlockSpec((1, d), fixed),
                  pl.BlockSpec((d, LANES), fixed), pl.BlockSpec((1, LANES), fixed)],
        out_specs=[pl.BlockSpec((tm, d), row), pl.BlockSpec((tm * (d // LANES), LANES), row),
                   pl.BlockSpec((tm, LANES), row)],
        out_shape=[jax.ShapeDtypeStruct((t, d), F32), jax.ShapeDtypeStruct((t * (d // LANES), LANES), F32),
                   jax.ShapeDtypeStruct((t, LANES), F32)],
        compiler_params=_params(("parallel",), 56),
        name="outproj_router",
    )(o_r, o_f, x, w_out_b, nw.reshape(1, d), w_r, b_r)


def _rms(x, nw):
    ms = jnp.mean(x * x, axis=-1, keepdims=True)
    return x * lax.rsqrt(ms + EPS) * nw


def _sample_proj_kernel(x_ref, nw_ref, w_ref, o_ref):
    o_ref[...] = _dot3(_rms(x_ref[...], nw_ref[...]), w_ref[0], _NT)


def _sample_logf_kernel(x_ref, nw_ref, w_ref, b_ref, o_ref):
    z = _dot3(_rms(x_ref[...], nw_ref[...]), w_ref[0], _NT) + b_ref[...]
    o_ref[...] = _log_sigmoid(z)


def _sample_proj(x, nw, w_t, layer, s_len, db, n_cols):
    d = x.shape[1]
    tn = _pick(n_cols, (512, 256, 128))
    return pl.pallas_call(
        _sample_proj_kernel,
        grid=(n_cols // tn,),
        in_specs=[pl.BlockSpec((db, d), lambda j: (s_len // db, 0)),
                  pl.BlockSpec((1, d), lambda j: (0, 0)),
                  pl.BlockSpec((1, tn, d), lambda j: (layer, j, 0))],
        out_specs=pl.BlockSpec((db, tn), lambda j: (0, j)),
        out_shape=jax.ShapeDtypeStruct((db, n_cols), F32),
        compiler_params=_params(("parallel",), 48),
        name="sample_proj",
    )(x, nw.reshape(1, d), w_t)


def _sample_logf(x, nw, w_t, layer, row0, bias, s_len, db):
    d = x.shape[1]
    n_out = bias.shape[1]
    return pl.pallas_call(
        _sample_logf_kernel,
        grid=(1,),
        in_specs=[pl.BlockSpec((db, d), lambda j: (s_len // db, 0)),
                  pl.BlockSpec((1, d), lambda j: (0, 0)),
                  pl.BlockSpec((1, n_out, d), lambda j: (layer, row0 // n_out, 0)),
                  pl.BlockSpec((1, n_out), lambda j: (0, 0))],
        out_specs=pl.BlockSpec((db, n_out), lambda j: (0, 0)),
        out_shape=jax.ShapeDtypeStruct((db, n_out), F32),
        compiler_params=_params(("arbitrary",), 40),
        name="sample_logf",
    )(x, nw.reshape(1, d), w_t, bias)


def _sample_outproj_kernel(cat_ref, w_ref, y_ref):
    y_ref[...] = _dot3(cat_ref[...], w_ref[0])


def _sample_outproj(cat, w_out, layer):
    db, width = cat.shape
    d = w_out.shape[2]
    tn = _pick(d, (512, 256, 128))
    return pl.pallas_call(
        _sample_outproj_kernel,
        grid=(d // tn,),
        in_specs=[pl.BlockSpec((db, width), lambda j: (0, 0)),
                  pl.BlockSpec((1, width, tn), lambda j: (layer, 0, j))],
        out_specs=pl.BlockSpec((db, tn), lambda j: (0, j)),
        out_shape=jax.ShapeDtypeStruct((db, d), F32),
        compiler_params=_params(("parallel",), 48),
        name="sample_outproj",
    )(cat, w_out)


def _sample_router_kernel(y_ref, x_ref, nw_ref, wr_ref, br_ref, x1_in, xg_in, route_in,
                          x1_ref, xg_ref, route_ref, *, n_groups, per_group):
    del x1_in, xg_in, route_in
    x1_ref[...], xn, route_ref[...] = _residual_norm_route(
        x_ref[...], y_ref[...], nw_ref[...], wr_ref[...], br_ref[...], n_groups, per_group)
    _store_row_slabs(xn, xg_ref)


def _sample_router(y, x, nw, w_r, b_r, x1, xg, route, s_len, n_groups, per_group):
    db, d = y.shape
    blk = s_len // db
    fixed = lambda i: (0, 0)
    rows = lambda i: (blk, 0)
    hbm = pl.BlockSpec(memory_space=pl.ANY)
    return pl.pallas_call(
        functools.partial(_sample_router_kernel, n_groups=n_groups, per_group=per_group),
        grid=(1,),
        in_specs=[pl.BlockSpec((db, d), fixed), pl.BlockSpec((db, d), rows), pl.BlockSpec((1, d), fixed),
                  pl.BlockSpec((d, LANES), fixed), pl.BlockSpec((1, LANES), fixed), hbm, hbm, hbm],
        out_specs=[pl.BlockSpec((db, d), rows), pl.BlockSpec((db * (d // LANES), LANES), rows),
                   pl.BlockSpec((db, LANES), rows)],
        out_shape=[jax.ShapeDtypeStruct(x1.shape, F32), jax.ShapeDtypeStruct(xg.shape, F32),
                   jax.ShapeDtypeStruct(route.shape, F32)],
        input_output_aliases={5: 0, 6: 1, 7: 2},
        compiler_params=_params(("arbitrary",), 40),
        name="sample_router",
    )(y, x, nw.reshape(1, d), w_r, b_r, x1, xg, route)


def _start_row_gather(src_of, hbm, buf, sem, rows, n, unrolled=False):
    pitch = _gather_pitch(n)

    def start(r, priority):
        src = pl.multiple_of(src_of(r) * n, n)
        dst = r * pitch if isinstance(r, int) else pl.multiple_of(r * pitch, 8)
        pltpu.make_async_copy(hbm.at[pl.ds(src, n)], buf.at[pl.ds(dst, n)], sem).start(priority=priority)

    if unrolled:
        for r in range(rows):
            start(r, r % 2)
        return

    def pair(h, carry):
        for u in range(2):
            start(2 * h + u, u)
        return carry

    lax.fori_loop(0, rows // 2, pair, 0)


def _wait_row_gather(hbm, buf, sem, rows, n):
    pltpu.make_async_copy(hbm.at[pl.ds(0, rows * n)], buf.at[pl.ds(0, rows * n)], sem).wait()


def _expert_kernel(te_ref, nu_ref, src_ref, x_hbm, w1_ref, w3_ref, w2_ref, y_ref, xbuf, sem):
    del te_ref
    i = pl.program_id(0)
    n_used = nu_ref[0]
    n = w1_ref.shape[2] // LANES
    tm = y_ref.shape[0] // n

    def fetch(tile, slot, unrolled):
        _start_row_gather(lambda r: src_ref[tile * tm + r], x_hbm, xbuf.at[slot], sem.at[slot], tm, n, unrolled)

    @pl.when(i == 0)
    def _():
        fetch(0, 0, False)

    @pl.when(i == n_used)
    def _():
        _wait_row_gather(x_hbm, xbuf.at[i % 2], sem.at[i % 2], tm, n)

    @pl.when(i < n_used)
    def _():
        slot = i % 2
        fetch(i + 1, 1 - slot, True)
        _wait_row_gather(x_hbm, xbuf.at[slot], sem.at[slot], tm, n)
        x = _load_row_slabs(xbuf.at[slot], tm, n, _gather_pitch(n)).astype(BF16)
        a = jnp.dot(x, w1_ref[0, 0], preferred_element_type=F32)
        b = jnp.dot(x, w3_ref[0, 0], preferred_element_type=F32)
        hmid = (_silu(a) * b).astype(BF16)
        _store_row_slabs(jnp.dot(hmid, w2_ref[0, 0], preferred_element_type=F32), y_ref)

    @pl.when(i >= n_used)
    def _():
        y_ref[...] = jnp.zeros_like(y_ref)


def _experts(xg, src, w1b, w3b, w2b, layer, tile_expert, n_used, tm, n_tiles):
    d, de = w1b.shape[2], w1b.shape[3]
    n = d // LANES

    def wsel(i, te, nu, s):
        return (layer, te[i], 0, 0)

    grid_spec = pltpu.PrefetchScalarGridSpec(
        num_scalar_prefetch=3,
        grid=(n_tiles,),
        in_specs=[pl.BlockSpec(memory_space=pl.ANY), pl.BlockSpec((1, 1, d, de), wsel),
                  pl.BlockSpec((1, 1, d, de), wsel), pl.BlockSpec((1, 1, de, d), wsel)],
        out_specs=pl.BlockSpec((tm * n, LANES), lambda i, te, nu, s: (i, 0)),
        scratch_shapes=[pltpu.VMEM((2, tm * _gather_pitch(n), LANES), F32), pltpu.SemaphoreType.DMA((2,))],
    )
    return pl.pallas_call(
        _expert_kernel,
        grid_spec=grid_spec,
        out_shape=jax.ShapeDtypeStruct((n_tiles * tm * n, LANES), F32),
        compiler_params=_params(("arbitrary",), 56),
        name="moe_experts",
    )(tile_expert, n_used, src, xg, w1b, w3b, w2b)


def _combine_kernel(pos_ref, x_ref, route_ref, nw_ref, y_hbm, *refs, rows, split_at):
    o_refs, (ybuf, sem) = refs[:-2], refs[-2:]
    i = pl.program_id(0)
    n = x_ref.shape[1] // LANES
    pitch = _gather_pitch(n)

    def fetch(tile, slot):
        for k in range(TOP_K_IN_GROUP):
            _start_row_gather(lambda r: pos_ref[TOP_K_IN_GROUP * (tile * rows + r) + k], y_hbm,
                              ybuf.at[slot, k], sem.at[slot, k], rows, n)

    @pl.when(i == 0)
    def _():
        fetch(0, 0)

    @pl.when(i + 1 < pl.num_programs(0))
    def _():
        fetch(i + 1, (i + 1) % 2)

    slot = i % 2
    for k in range(TOP_K_IN_GROUP):
        _wait_row_gather(y_hbm, ybuf.at[slot, k], sem.at[slot, k], rows, n)
    route = route_ref[...]
    wt0 = jnp.broadcast_to(route[:, 2:3], (rows, LANES))
    wt1 = jnp.broadcast_to(route[:, 3:4], (rows, LANES))
    moe = jnp.concatenate([wt0 * ybuf[slot, 0, pl.ds(s, rows, stride=pitch), :]
                           + wt1 * ybuf[slot, 1, pl.ds(s, rows, stride=pitch), :] for s in range(n)], axis=1)
    x2 = x_ref[...] + moe
    ms = jnp.mean(x2 * x2, axis=-1, keepdims=True)
    xn = x2 * lax.rsqrt(ms + EPS) * nw_ref[...]
    if split_at is None:
        x2_ref, xn_ref = o_refs
        x2_ref[...] = x2
        xn_ref[...] = xn.astype(xn_ref.dtype)
    else:
        yp_ref, ys_ref = o_refs

        @pl.when(i < split_at)
        def _():
            yp_ref[...] = xn

        @pl.when(i >= split_at)
        def _():
            ys_ref[...] = xn


def _combine(x1, route, y_sorted, pos_flat, nw, split_rows=None):
    t, d = x1.shape
    rows = _pick(t, (128, 64, 8))
    row = lambda i, p: (i, 0)
    if split_rows is None:
        split_at = None
        out_specs = [pl.BlockSpec((rows, d), row), pl.BlockSpec((rows, d), row)]
        out_shape = [jax.ShapeDtypeStruct((t, d), F32), jax.ShapeDtypeStruct((t, d), BF16)]
    else:
        split_at = split_rows // rows
        out_specs = [pl.BlockSpec((rows, d), lambda i, p: (jnp.minimum(i, split_at - 1), 0)),
                     pl.BlockSpec((rows, d), lambda i, p: (jnp.maximum(i - split_at, 0), 0))]
        out_shape = [jax.ShapeDtypeStruct((split_rows, d), F32), jax.ShapeDtypeStruct((t - split_rows, d), F32)]
    grid_spec = pltpu.PrefetchScalarGridSpec(
        num_scalar_prefetch=1,
        grid=(t // rows,),
        in_specs=[pl.BlockSpec((rows, d), row), pl.BlockSpec((rows, LANES), row),
                  pl.BlockSpec((1, d), lambda i, p: (0, 0)), pl.BlockSpec(memory_space=pl.ANY)],
        out_specs=out_specs,
        scratch_shapes=[pltpu.VMEM((2, TOP_K_IN_GROUP, rows * _gather_pitch(d // LANES), LANES), F32),
                        pltpu.SemaphoreType.DMA((2, TOP_K_IN_GROUP))],
    )
    return pl.pallas_call(
        functools.partial(_combine_kernel, rows=rows, split_at=split_at),
        grid_spec=grid_spec,
        out_shape=out_shape,
        compiler_params=_params(("arbitrary",), 40),
        name="moe_combine",
    )(pos_flat, x1, route, nw.reshape(1, d), y_sorted)


def _routing_tables(e_id, n_experts, tm, n_tiles):
    flat = e_id.reshape(-1)
    onehot = (flat[:, None] == jnp.arange(n_experts, dtype=jnp.int32)[None, :]).astype(jnp.int32)
    csum = jnp.cumsum(onehot, axis=0)
    rank = jnp.sum((csum - onehot) * onehot, axis=1)
    counts = csum[-1]
    padded = ((counts + tm - 1) // tm) * tm
    ends = jnp.cumsum(padded)
    starts = ends - padded
    pos = (jnp.sum(onehot * starts[None, :], axis=1) + rank).astype(jnp.int32)
    token = jnp.arange(flat.shape[0], dtype=jnp.int32) // TOP_K_IN_GROUP
    src = jnp.zeros((n_tiles * tm,), jnp.int32).at[pos].set(token)
    tile_start = jnp.arange(n_tiles, dtype=jnp.int32) * tm
    n_used = (ends[-1] // tm).astype(jnp.int32)
    tile_expert = jnp.sum((tile_start[:, None] >= ends[None, :]).astype(jnp.int32), axis=1)
    last_expert = jnp.sum((jnp.maximum(ends[-1] - 1, 0) >= ends).astype(jnp.int32))
    tile_expert = jnp.minimum(tile_expert, last_expert).astype(jnp.int32)
    return src, pos, tile_expert, n_used.reshape(1)


def kernel(x_prompt, x_sample, cache_k, cache_v, cache_logf, state_ret, page_table, norm_mix_w, w_in, b_fgt,
           ret_gn_w, w_out, norm_ffn_w, w_group, b_group, w_expert, b_expert, w1, w3, w2, norm_final_w):
    assert x_prompt.shape[0] == 1 and x_sample.shape[1] == 1
    s_len, d = x_prompt.shape[1], x_prompt.shape[2]
    db = x_sample.shape[0]
    depth = w_in.shape[0]
    n_pool, page, h_fox = cache_k.shape[1], cache_k.shape[2], cache_k.shape[3]
    h_ret = state_ret.shape[2]
    w_ret, w_fox = h_ret * HEAD_DIM, h_fox * HEAD_DIM
    n_pages = page_table.shape[1]
    past_len = n_pages * page
    n_groups, n_experts = w_group.shape[-1], w_expert.shape[-1]
    per_group = n_experts // n_groups
    t = s_len + db
    chunk = 128
    tq_f = _pick(s_len, (512, 256, 128))
    tm_e = 256
    n_tiles = (TOP_K_IN_GROUP * t + n_experts * (tm_e - 1) + tm_e - 1) // tm_e + 1

    inv_freq = ROPE_BASE ** (-jnp.arange(0, HEAD_DIM, 2, dtype=F32) / HEAD_DIM)
    posv = jnp.concatenate([jnp.arange(s_len), jnp.full((db,), past_len)]).astype(F32)
    ang = posv[:, None] * inv_freq[None, :]
    cos2 = jnp.concatenate([jnp.cos(ang), jnp.cos(ang)], axis=-1)
    sin2 = jnp.concatenate([-jnp.sin(ang), jnp.sin(ang)], axis=-1)

    cache_k4 = cache_k.reshape(depth, n_pool, page * h_fox, HEAD_DIM)
    cache_v4 = cache_v.reshape(depth, n_pool, page * h_fox, HEAD_DIM)
    pt_flat = page_table.reshape(-1)
    assert w_ret == w_fox
    w_t = jnp.swapaxes(w_in, 1, 2)

    w1b, w3b, w2b = w1.astype(BF16), w3.astype(BF16), w2.astype(BF16)

    x = jnp.concatenate([x_prompt[0], x_sample[:, 0]], axis=0)
    xn = _rmsnorm(x, norm_mix_w[0], BF16)

    k_p, v_p, lf_p, st_p, k_s, v_s, lf_s, st_s = [], [], [], [], [], [], [], []
    for l in range(depth):
        qk = _proj_rope(xn, w_t, l, cos2, sin2, w_ret, s_len)
        vg, = _proj(xn, w_t, l, 2, 2, (F32,), w_ret, s_len, name="proj_vg")
        c0 = 4 * w_ret
        qf, = _proj(xn, w_t, l, 4, 1, (BF16,), w_ret, s_len, scale=HEAD_DIM ** -0.5 * LOG2E, name="proj_qf")
        kf, kb = _proj(xn, w_t, l, 5, 1, (F32, BF16), w_ret, s_len, name="proj_k")
        vf, vb = _proj(xn, w_t, l, 6, 1, (F32, BF16), w_ret, s_len, name="proj_v")
        b_f = b_fgt[l].reshape(1, h_fox)
        logf = _proj_logf(xn, w_t, l, c0 + 3 * w_fox, b_f, s_len)
        o_r_p, s_fin = _retention_prompt(qk, vg, ret_gn_w[l], s_len, w_ret, chunk)
        c_hi, c_mid, c_lo = (p[:, :, None] for p in _split3(jnp.cumsum(logf, axis=0) * LOG2E))
        one = jnp.ones_like(c_hi)
        per_head = lambda a: a.reshape(s_len, h_fox, HEAD_DIM)
        pad_to = lambda n, *cols: jnp.concatenate(cols + (jnp.zeros((s_len, h_fox, n - sum(c.shape[2] for c in cols)), BF16),), axis=2)
        to_blocks = lambda a: a.reshape(s_len // tq_f, tq_f, h_fox, a.shape[2]).transpose(2, 0, 3, 1)
        ka = pad_to(AUG, per_head(kb), c_hi, c_mid, c_lo, one, one, one).reshape(s_len, h_fox * AUG)
        qa = to_blocks(pad_to(AUG, per_head(qf), -one, -one, -one, c_hi, c_mid, c_lo))
        v1 = to_blocks(pad_to(V_ROWS, per_head(vb), one))
        o_f_p = _fox_prompt_t(qa, ka, v1, s_len, w_fox)

        nw_mix = norm_mix_w[l]
        p_s = _sample_proj(x, nw_mix, w_t, l, s_len, db, c0 + 3 * w_fox)
        logf_s = _sample_logf(x, nw_mix, w_t, l, c0 + 3 * w_fox, b_f, s_len, db)
        heads3 = lambda w0, h: p_s[:, w0:w0 + h * HEAD_DIM].reshape(db, h, HEAD_DIM)
        o_r_s, s_new = _retention_sample(heads3(0, h_ret), heads3(w_ret, h_ret), heads3(2 * w_ret, h_ret),
                                         heads3(3 * w_ret, h_ret), cos2[s_len:s_len + 1], sin2[s_len:s_len + 1],
                                         ret_gn_w[l], state_ret, l)
        lf_past = cache_logf[l][page_table].reshape(db, past_len, h_fox)
        suffix = jnp.flip(jnp.cumsum(jnp.flip(lf_past, axis=1), axis=1), axis=1) - lf_past
        bias = (suffix + logf_s[:, None, :]).reshape(db, n_pages, page * h_fox)
        o_f_s = _fox_sample(heads3(c0, h_fox), heads3(c0 + w_fox, h_fox), heads3(c0 + 2 * w_fox, h_fox), bias,
                            cache_k4, cache_v4, pt_flat, l, n_pages)
        cat_s = jnp.concatenate([o_r_s.reshape(db, w_ret), o_f_s.reshape(db, w_fox)], axis=1)

        o_r = jnp.concatenate([o_r_p, cat_s[:, :w_ret].astype(BF16)], axis=0)
        o_f = jnp.concatenate([o_f_p, cat_s[:, w_ret:].astype(BF16)], axis=0)

        w_r = jnp.zeros((d, LANES), F32).at[:, :n_groups].set(w_group[l]).at[:, n_groups:n_groups + n_experts].set(w_expert[l])
        b_r = jnp.zeros((1, LANES), F32).at[0, :n_groups].set(b_group[l]).at[0, n_groups:n_groups + n_experts].set(b_expert[l])
        x1, xg, route = _outproj(o_r, o_f, x, w_out[l].astype(BF16), norm_ffn_w[l], w_r, b_r, n_groups, per_group)
        y_s = _sample_outproj(cat_s, w_out, l)
        x1, xg, route = _sample_router(y_s, x, norm_ffn_w[l], w_r, b_r, x1, xg, route, s_len, n_groups, per_group)

        e_id = route[:, :TOP_K_IN_GROUP].astype(jnp.int32)
        src, pos, tile_expert, n_used = _routing_tables(e_id, n_experts, tm_e, n_tiles)
        y_sorted = _experts(xg, src, w1b, w3b, w2b, l, tile_expert, n_used, tm_e, n_tiles)
        if l < depth - 1:
            x, xn = _combine(x1, route, y_sorted, pos, norm_mix_w[l + 1])
        else:
            y_prompt, y_sample = _combine(x1, route, y_sorted, pos, norm_final_w, split_rows=s_len)

        k_p.append(kf.reshape(1, s_len, h_fox, HEAD_DIM))
        v_p.append(vf.reshape(1, s_len, h_fox, HEAD_DIM))
        lf_p.append(logf.reshape(1, s_len, h_fox))
        st_p.append(s_fin.reshape(1, h_ret, HEAD_DIM, HEAD_DIM))
        k_s.append(heads3(c0 + w_fox, h_fox).reshape(db, 1, h_fox, HEAD_DIM))
        v_s.append(heads3(c0 + 2 * w_fox, h_fox).reshape(db, 1, h_fox, HEAD_DIM))
        lf_s.append(logf_s.reshape(db, 1, h_fox))
        st_s.append(s_new)

    return (y_prompt.reshape(1, s_len, d), y_sample.reshape(db, 1, d), jnp.stack(k_p), jnp.stack(v_p), jnp.stack(lf_p), jnp.stack(st_p),
            jnp.stack(k_s), jnp.stack(v_s), jnp.stack(lf_s), jnp.stack(st_s))
```

```python
import functools

import jax
import jax.numpy as jnp
from jax import lax
from jax.experimental import pallas as pl
from jax.experimental.pallas import tpu as pltpu

F32 = jnp.float32
BF16 = jnp.bfloat16

HEAD_DIM = 128
EPS = 1e-6
ROPE_BASE = 10000.0
TOP_K_IN_GROUP = 2
LANES = 128
MIB = 1 << 20
NEG_INF = float("-inf")
LOG2E = 1.4426950408889634


def _pick(n, cands):
    for c in cands:
        if n % c == 0:
            return c
    raise ValueError(f"no tile for {n} in {cands}")


def _params(sem, vmem_mib):
    return pltpu.CompilerParams(dimension_semantics=sem, vmem_limit_bytes=vmem_mib * MIB)


def _silu(x):
    return x * (1.0 / (1.0 + jnp.exp(-x)))


def _split(a):
    hi = a.astype(BF16)
    return hi, (a - hi.astype(F32)).astype(BF16)


_NN = (((1,), (0,)), ((), ()))
_NT = (((1,), (1,)), ((), ()))


def _dot3(a, b, dims=_NN):
    ah, al = _split(a)
    bh, bl = _split(b)
    dot = lambda u, v: lax.dot_general(u, v, dims, preferred_element_type=F32)
    m = a.shape[0]
    top = dot(jnp.concatenate([ah, al], axis=0), bh)
    return top[:m] + top[m:] + dot(ah, bl)


def _rmsnorm_kernel(x_ref, w_ref, o_ref):
    x = x_ref[...]
    ms = jnp.mean(x * x, axis=-1, keepdims=True)
    o_ref[...] = (x * lax.rsqrt(ms + EPS) * w_ref[...]).astype(o_ref.dtype)


def _rmsnorm(x, w, out_dtype):
    t, d = x.shape
    tm = _pick(t, (640, 320, 256, 128, 64, 8))
    return pl.pallas_call(
        _rmsnorm_kernel,
        grid=(t // tm,),
        in_specs=[pl.BlockSpec((tm, d), lambda i: (i, 0)), pl.BlockSpec((1, d), lambda i: (0, 0))],
        out_specs=pl.BlockSpec((tm, d), lambda i: (i, 0)),
        out_shape=jax.ShapeDtypeStruct((t, d), out_dtype),
        compiler_params=_params(("parallel",), 40),
        name="rmsnorm",
    )(x, w.reshape(1, d))


def _load_weight_block(w_ref, wb_ref):
    step = 2 * LANES

    @pl.when(pl.program_id(1) == 0)
    def _():
        for c in range(w_ref.shape[1] // step):
            wb_ref[:, c * step:(c + 1) * step] = w_ref[0, c * step:(c + 1) * step, :].T.astype(BF16)


def _proj_kernel(x_ref, w_ref, *refs, scale):
    o_refs, wb_ref = refs[:-1], refs[-1]
    _load_weight_block(w_ref, wb_ref)
    acc = jnp.dot(x_ref[...], wb_ref[...], preferred_element_type=F32)
    if scale != 1.0:
        acc = acc * scale
    for o_ref in o_refs:
        o_ref[...] = acc.astype(o_ref.dtype)


def _proj_rope_kernel(x_ref, w_ref, cos_ref, sin_ref, o_ref, wb_ref, *, heads, kscale):
    j = pl.program_id(0)
    _load_weight_block(w_ref, wb_ref)
    acc = jnp.dot(x_ref[...], wb_ref[...], preferred_element_type=F32)
    cos = cos_ref[...]
    sin = sin_ref[...]
    scale = jnp.where(j == 1, kscale, 1.0).astype(F32)
    for h in range(heads):
        sl = slice(h * HEAD_DIM, (h + 1) * HEAD_DIM)
        xh = acc[:, sl]
        r = xh * cos + pltpu.roll(xh, HEAD_DIM // 2, axis=1) * sin
        o_ref[:, sl] = (r * scale).astype(o_ref.dtype)


AUG = 2 * HEAD_DIM
V_ROWS = HEAD_DIM + 16


def _split3(c):
    hi = c.astype(BF16).astype(F32)
    mid = (c - hi).astype(BF16).astype(F32)
    return hi, mid, (c - hi - mid).astype(BF16).astype(F32)


def _proj_k_aug_kernel(x_ref, w_ref, c_ref, kf_ref, ka_ref, wb_ref, *, heads):
    _load_weight_block(w_ref, wb_ref)
    acc = jnp.dot(x_ref[...], wb_ref[...], preferred_element_type=F32)
    kf_ref[...] = acc
    c = c_ref[...]
    lane = lax.broadcasted_iota(jnp.int32, (acc.shape[0], HEAD_DIM), 1)
    for h in range(heads):
        hi, mid, lo = _split3(c[:, h:h + 1])
        aug = jnp.where(lane == 0, hi, jnp.where(lane == 1, mid, jnp.where(lane == 2, lo,
                                                                            jnp.where(lane < 6, 1.0, 0.0))))
        ka_ref[:, h * AUG:h * AUG + HEAD_DIM] = acc[:, h * HEAD_DIM:(h + 1) * HEAD_DIM].astype(BF16)
        ka_ref[:, h * AUG + HEAD_DIM:(h + 1) * AUG] = aug.astype(BF16)


def _proj_t_kernel(x_ref, w_ref, *refs, scale, with_bias, heads):
    o_ref, wb_ref = refs[-2], refs[-1]

    @pl.when(pl.program_id(0) == 0)
    def _():
        wb_ref[...] = w_ref[0].astype(BF16)

    t = lax.dot_general(wb_ref[...], x_ref[...], _NT, preferred_element_type=F32)
    n_aug = o_ref.shape[2] - HEAD_DIM
    sub = lax.broadcasted_iota(jnp.int32, (n_aug, t.shape[1]), 0)
    for h in range(heads):
        o_ref[h, 0, :HEAD_DIM, :] = (t[h * HEAD_DIM:(h + 1) * HEAD_DIM, :] * scale).astype(BF16)
        if with_bias:
            hi, mid, lo = _split3(refs[0][h:h + 1, :])
            aug = jnp.where(sub < 3, -1.0, jnp.where(sub == 3, hi, jnp.where(sub == 4, mid,
                                                                             jnp.where(sub == 5, lo, 0.0))))
        else:
            aug = jnp.where(sub == 0, 1.0, 0.0)
        o_ref[h, 0, HEAD_DIM:, :] = aug.astype(BF16)


def _log_sigmoid(z):
    return jnp.minimum(z, 0.0) - jnp.log1p(jnp.exp(-jnp.abs(z)))


def _proj_logf_kernel(x_ref, w_ref, b_ref, o_ref):
    z = lax.dot_general(x_ref[...], w_ref[0].astype(BF16), _NT, preferred_element_type=F32)
    o_ref[...] = _log_sigmoid(z + b_ref[...])


def _proj(xn, w_t, layer, blk0, nblk, out_dtypes, tn, t, scale=1.0, name="proj"):
    d = xn.shape[1]
    tm = _pick(t, (512, 256, 128, 64, 8))
    outs = pl.pallas_call(
        functools.partial(_proj_kernel, scale=scale),
        grid=(nblk, t // tm),
        in_specs=[pl.BlockSpec((tm, d), lambda j, i: (i, 0)),
                  pl.BlockSpec((1, tn, d), lambda j, i: (layer, blk0 + j, 0))],
        out_specs=[pl.BlockSpec((tm, tn), lambda j, i: (i, j)) for _ in out_dtypes],
        out_shape=[jax.ShapeDtypeStruct((t, nblk * tn), dt) for dt in out_dtypes],
        scratch_shapes=[pltpu.VMEM((d, tn), BF16)],
        compiler_params=_params(("arbitrary", "arbitrary"), 56),
        name=name,
    )(xn, w_t)
    return outs


def _proj_rope(xn, w_t, layer, cos2, sin2, w_ret, t):
    d = xn.shape[1]
    tm = _pick(t, (512, 256, 128, 64, 8))
    return pl.pallas_call(
        functools.partial(_proj_rope_kernel, heads=w_ret // HEAD_DIM, kscale=HEAD_DIM ** -0.5),
        grid=(2, t // tm),
        in_specs=[pl.BlockSpec((tm, d), lambda j, i: (i, 0)),
                  pl.BlockSpec((1, w_ret, d), lambda j, i: (layer, j, 0)),
                  pl.BlockSpec((tm, HEAD_DIM), lambda j, i: (i, 0)),
                  pl.BlockSpec((tm, HEAD_DIM), lambda j, i: (i, 0))],
        out_specs=pl.BlockSpec((tm, w_ret), lambda j, i: (i, j)),
        out_shape=jax.ShapeDtypeStruct((t, 2 * w_ret), BF16),
        scratch_shapes=[pltpu.VMEM((d, w_ret), BF16)],
        compiler_params=_params(("arbitrary", "arbitrary"), 56),
        name="proj_rope",
    )(xn, w_t, cos2, sin2)


def _proj_k_aug(xn, w_t, layer, blk, c, w_fox):
    t, heads = c.shape
    d = xn.shape[1]
    tm = _pick(t, (512, 256, 128, 64, 8))
    return pl.pallas_call(
        functools.partial(_proj_k_aug_kernel, heads=heads),
        grid=(1, t // tm),
        in_specs=[pl.BlockSpec((tm, d), lambda j, i: (i, 0)),
                  pl.BlockSpec((1, w_fox, d), lambda j, i: (layer, blk, 0)),
                  pl.BlockSpec((tm, heads), lambda j, i: (i, 0))],
        out_specs=[pl.BlockSpec((tm, w_fox), lambda j, i: (i, 0)), pl.BlockSpec((tm, heads * AUG), lambda j, i: (i, 0))],
        out_shape=[jax.ShapeDtypeStruct((t, w_fox), F32), jax.ShapeDtypeStruct((t, heads * AUG), BF16)],
        scratch_shapes=[pltpu.VMEM((d, w_fox), BF16)],
        compiler_params=_params(("arbitrary", "arbitrary"), 56),
        name="proj_k_aug",
    )(xn, w_t, c)


def _proj_t(xn, w_t, layer, row0, heads, tq, t, c_t=None, scale=1.0, name="proj_t"):
    d = xn.shape[1]
    rows = AUG if c_t is not None else V_ROWS
    width = heads * HEAD_DIM
    in_specs = [pl.BlockSpec((tq, d), lambda i: (i, 0)),
                pl.BlockSpec((1, width, d), lambda i: (layer, row0 // width, 0))]
    args = [xn, w_t]
    if c_t is not None:
        in_specs.append(pl.BlockSpec((heads, tq), lambda i: (0, i)))
        args.append(c_t)
    return pl.pallas_call(
        functools.partial(_proj_t_kernel, scale=scale, with_bias=c_t is not None, heads=heads),
        grid=(t // tq,),
        in_specs=in_specs,
        out_specs=pl.BlockSpec((heads, 1, rows, tq), lambda i: (0, i, 0, 0)),
        out_shape=jax.ShapeDtypeStruct((heads, t // tq, rows, tq), BF16),
        scratch_shapes=[pltpu.VMEM((width, d), BF16)],
        compiler_params=_params(("arbitrary",), 56),
        name=name,
    )(*args)


def _proj_logf(xn, w_t, layer, row0, bias, t):
    d = xn.shape[1]
    n_out = bias.shape[1]
    tm = _pick(t, (512, 256, 128, 64, 8))
    return pl.pallas_call(
        _proj_logf_kernel,
        grid=(t // tm,),
        in_specs=[pl.BlockSpec((tm, d), lambda i: (i, 0)),
                  pl.BlockSpec((1, n_out, d), lambda i: (layer, row0 // n_out, 0)),
                  pl.BlockSpec((1, n_out), lambda i: (0, 0))],
        out_specs=pl.BlockSpec((tm, n_out), lambda i: (i, 0)),
        out_shape=jax.ShapeDtypeStruct((t, n_out), F32),
        compiler_params=_params(("parallel",), 40),
        name="proj_logf",
    )(xn, w_t, bias)


def _groupnorm_gate(o, g, gnw):
    mu = jnp.mean(o, axis=-1, keepdims=True)
    oc = o - mu
    var = jnp.mean(oc * oc, axis=-1, keepdims=True)
    return _silu(g) * (oc * lax.rsqrt(var + EPS) * gnw)


def _ret_prompt_kernel(q_ref, k_ref, v_ref, g_ref, gnw_ref, decay_ref, xi_ref, zeta_ref, gc_ref,
                       o_ref, state_ref, *, heads):
    c = pl.program_id(0)

    @pl.when(c == 0)
    def _():
        state_ref[...] = jnp.zeros_like(state_ref)

    for h in range(heads):
        sl = slice(h * HEAD_DIM, (h + 1) * HEAD_DIM)
        q = q_ref[:, sl]
        k = k_ref[:, sl]
        v = v_ref[:, sl]
        st = state_ref[h]
        s = lax.dot_general(q, k, (((1,), (1,)), ((), ())), preferred_element_type=F32) * decay_ref[h]
        o = jnp.dot(s.astype(BF16), v.astype(BF16), preferred_element_type=F32)
        o = o + jnp.dot(q, st.astype(BF16), preferred_element_type=F32) * xi_ref[h]
        vz = (v * zeta_ref[h]).astype(BF16)
        kv = lax.dot_general(k, vz, (((0,), (0,)), ((), ())), preferred_element_type=F32)
        state_ref[h] = gc_ref[h] * st + kv
        o_ref[:, sl] = _groupnorm_gate(o, g_ref[:, sl], gnw_ref[:, sl]).astype(o_ref.dtype)


def _retention_tables(heads, chunk):
    lg = jnp.log1p(-jnp.exp2(-5.0 - jnp.arange(heads, dtype=F32)))
    idx = jnp.arange(chunk, dtype=F32)
    diff = idx[:, None] - idx[None, :]
    decay = jnp.where(diff >= 0, jnp.exp(lg[:, None, None] * jnp.maximum(diff, 0.0)), 0.0)
    xi = jnp.exp(lg[:, None] * (idx[None, :] + 1.0))
    zeta = jnp.exp(lg[:, None] * (chunk - 1.0 - idx[None, :]))
    ones = jnp.ones((1, 1, HEAD_DIM), F32)
    return decay, xi[:, :, None] * ones, zeta[:, :, None] * ones, jnp.exp(lg * chunk)[:, None, None] * ones


def _retention_prompt(qk, vg, gnw, s_len, w_ret, chunk):
    heads = w_ret // HEAD_DIM
    decay, xi, zeta, gamma_c = _retention_tables(heads, chunk)
    tbl = pl.BlockSpec((heads, chunk, HEAD_DIM), lambda c: (0, 0, 0))
    return pl.pallas_call(
        functools.partial(_ret_prompt_kernel, heads=heads),
        grid=(s_len // chunk,),
        in_specs=[pl.BlockSpec((chunk, w_ret), lambda c: (c, 0)),
                  pl.BlockSpec((chunk, w_ret), lambda c: (c, 1)),
                  pl.BlockSpec((chunk, w_ret), lambda c: (c, 0)),
                  pl.BlockSpec((chunk, w_ret), lambda c: (c, 1)),
                  pl.BlockSpec((1, w_ret), lambda c: (0, 0)),
                  tbl, tbl, tbl, pl.BlockSpec((heads, 1, HEAD_DIM), lambda c: (0, 0, 0))],
        out_specs=[pl.BlockSpec((chunk, w_ret), lambda c: (c, 0)),
                   pl.BlockSpec((heads, HEAD_DIM, HEAD_DIM), lambda c: (0, 0, 0))],
        out_shape=[jax.ShapeDtypeStruct((s_len, w_ret), BF16),
                   jax.ShapeDtypeStruct((heads, HEAD_DIM, HEAD_DIM), F32)],
        compiler_params=_params(("arbitrary",), 40),
        name="retention_prompt",
    )(qk, qk, vg, vg, gnw.reshape(1, w_ret), decay, xi, zeta, gamma_c)


def _ret_sample_kernel(q_ref, k_ref, v_ref, g_ref, cos_ref, sin_ref, gnw_ref, gamma_ref, st_ref, o_ref, ns_ref, *,
                       heads):
    cos = cos_ref[...]
    sin = sin_ref[...]
    rope = lambda a: a * cos + pltpu.roll(a, HEAD_DIM // 2, axis=1) * sin
    pad = jnp.zeros((HEAD_DIM - 2 * heads, HEAD_DIM), F32)
    for b in range(q_ref.shape[0]):
        q = rope(q_ref[b])
        k = rope(k_ref[b]) * (HEAD_DIM ** -0.5)
        v = v_ref[b]
        qk = jnp.sum(q * k, axis=-1, keepdims=True)
        cols = jnp.concatenate([q, k, pad], axis=0).T
        rows = []
        for h in range(heads):
            st = st_ref[0, b, h]
            qcol = cols[:, h:h + 1]
            kcol = cols[:, heads + h:heads + h + 1]
            vrow = v[h:h + 1, :]
            gamma = gamma_ref[h]
            rows.append(gamma * jnp.sum(st * qcol, axis=0, keepdims=True) + qk[h:h + 1, :] * vrow)
            ns_ref[b, h] = gamma * st + kcol * vrow
        o = jnp.concatenate(rows, axis=0)
        o_ref[b] = _groupnorm_gate(o, g_ref[b], gnw_ref[...]).astype(o_ref.dtype)


def _retention_sample(q3, k3, v3, g3, cos_row, sin_row, gnw, state_ret, layer):
    db, heads, _ = q3.shape
    gamma = _retention_tables(heads, 1)[3]
    sb = _pick(db, (4, 2, 1))
    vec = pl.BlockSpec((sb, heads, HEAD_DIM), lambda b: (b, 0, 0))
    row = pl.BlockSpec((1, HEAD_DIM), lambda b: (0, 0))
    return pl.pallas_call(
        functools.partial(_ret_sample_kernel, heads=heads),
        grid=(db // sb,),
        in_specs=[vec, vec, vec, vec, row, row,
                  pl.BlockSpec((heads, HEAD_DIM), lambda b: (0, 0)),
                  pl.BlockSpec((heads, 1, HEAD_DIM), lambda b: (0, 0, 0)),
                  pl.BlockSpec((1, sb, heads, HEAD_DIM, HEAD_DIM), lambda b: (layer, b, 0, 0, 0))],
        out_specs=[vec, pl.BlockSpec((sb, heads, HEAD_DIM, HEAD_DIM), lambda b: (b, 0, 0, 0))],
        out_shape=[jax.ShapeDtypeStruct((db, heads, HEAD_DIM), F32),
                   jax.ShapeDtypeStruct((db, heads, HEAD_DIM, HEAD_DIM), F32)],
        compiler_params=_params(("parallel",), 40),
        name="retention_sample",
    )(q3, k3, v3, g3, cos_row, sin_row, gnw.reshape(heads, HEAD_DIM), gamma, state_ret)


def _fox_prompt_t_kernel(qa_ref, ka_ref, v1_ref, o_ref, m_ref, acc_ref, *, tq, col_chunks):
    qi = pl.program_id(1)
    tc = tq // col_chunks
    m_ref[...] = jnp.full_like(m_ref, NEG_INF)
    acc_ref[...] = jnp.zeros_like(acc_ref)

    def weights(kj, masked):
        start = pl.multiple_of(kj * tq, tq)
        ka = ka_ref[pl.ds(start, tq), :]
        ps, alphas = [], []
        for c in range(col_chunks):
            cols = pl.ds(c * tc, tc)
            t = jnp.dot(ka, qa_ref[0, 0, :, cols], preferred_element_type=F32)
            if masked:
                key = lax.broadcasted_iota(jnp.int32, (tq, tc), 0)
                qry = lax.broadcasted_iota(jnp.int32, (tq, tc), 1) + c * tc
                t = jnp.where(key <= qry, t, NEG_INF)
            m_prev = m_ref[:, cols]
            m_new = jnp.maximum(m_prev, jnp.max(t, axis=0, keepdims=True))
            m_ref[:, cols] = m_new
            ps.append(jnp.exp2(t - m_new).astype(BF16))
            alphas.append(jnp.exp2(m_prev - m_new))
        return jnp.concatenate(ps, axis=1), jnp.concatenate(alphas, axis=1)

    def accumulate(kj, p, alpha):
        acc_ref[...] = alpha * acc_ref[...] + jnp.dot(v1_ref[0, kj], p, preferred_element_type=F32)

    def body(kj, carry):
        accumulate(jnp.maximum(kj - 1, 0), *carry)
        return weights(kj, False)

    carry = lax.fori_loop(0, qi, body, (jnp.zeros((tq, tq), BF16), jnp.ones((1, tq), F32)))
    accumulate(jnp.maximum(qi - 1, 0), *carry)
    accumulate(qi, *weights(qi, True))
    o_t = acc_ref[:HEAD_DIM, :] / acc_ref[HEAD_DIM:HEAD_DIM + 1, :]
    o_ref[...] = o_t.T.astype(o_ref.dtype)


def _fox_prompt_t(qa, ka, v1, s_len, w_fox):
    heads, nq, _, tq = qa.shape
    return pl.pallas_call(
        functools.partial(_fox_prompt_t_kernel, tq=tq, col_chunks=2),
        grid=(heads, nq),
        in_specs=[pl.BlockSpec((1, 1, AUG, tq), lambda h, qi: (h, qi, 0, 0)),
                  pl.BlockSpec((s_len, AUG), lambda h, qi: (0, h)),
                  pl.BlockSpec((1, nq, V_ROWS, tq), lambda h, qi: (h, 0, 0, 0))],
        out_specs=pl.BlockSpec((tq, HEAD_DIM), lambda h, qi: (qi, h)),
        out_shape=jax.ShapeDtypeStruct((s_len, w_fox), BF16),
        scratch_shapes=[pltpu.VMEM((1, tq), F32), pltpu.VMEM((V_ROWS, tq), F32)],
        compiler_params=_params(("parallel", "arbitrary"), 40),
        name="fox_prompt",
    )(qa, ka, v1)


def _fox_sample_kernel(pt_ref, q_ref, kn_ref, vn_ref, bias_ref, *refs, heads, pages_per_step):
    del pt_ref
    k_refs = refs[:pages_per_step]
    v_refs = refs[pages_per_step:2 * pages_per_step]
    o_ref, m_ref, l_ref, acc_ref = refs[2 * pages_per_step:]
    jg = pl.program_id(1)
    q = q_ref[0] * (HEAD_DIM ** -0.5)

    @pl.when(jg == 0)
    def _():
        m_ref[...] = jnp.sum(q * kn_ref[0], axis=-1, keepdims=True)
        l_ref[...] = jnp.ones_like(l_ref)
        acc_ref[...] = vn_ref[0]

    rows = k_refs[0].shape[2]
    sub = lax.broadcasted_iota(jnp.int32, (heads, rows), 0)
    lane = lax.broadcasted_iota(jnp.int32, (heads, rows), 1)
    own_head = (lane % heads) == sub
    s = []
    for g in range(pages_per_step):
        sg = _dot3(q, k_refs[g][0, 0], _NT)
        sg = sg + bias_ref[0, pl.ds(jg * pages_per_step + g, 1), :]
        s.append(jnp.where(own_head, sg, NEG_INF))
    m_prev = m_ref[...]
    m_new = m_prev
    for sg in s:
        m_new = jnp.maximum(m_new, jnp.max(sg, axis=-1, keepdims=True))
    alpha = jnp.exp(m_prev - m_new)
    l_new = alpha * l_ref[...]
    acc = alpha * acc_ref[...]
    for g, sg in enumerate(s):
        p = jnp.exp(sg - m_new)
        l_new = l_new + jnp.sum(p, axis=-1, keepdims=True)
        acc = acc + _dot3(p, v_refs[g][0, 0])
    l_ref[...] = l_new
    acc_ref[...] = acc
    m_ref[...] = m_new

    @pl.when(jg == pl.num_programs(1) - 1)
    def _():
        o_ref[0] = (acc_ref[...] / l_ref[...]).astype(o_ref.dtype)


def _fox_sample(q3, kn3, vn3, bias, cache_k4, cache_v4, pt_flat, layer, n_pages):
    db, heads, _ = q3.shape
    rows = cache_k4.shape[2]
    g = _pick(n_pages, (8, 4, 2, 1))
    vec = pl.BlockSpec((1, heads, HEAD_DIM), lambda b, jg, pt: (b, 0, 0))

    def page_spec(i):
        return pl.BlockSpec((1, 1, rows, HEAD_DIM),
                            lambda b, jg, pt: (layer, pt[b * n_pages + jg * g + i], 0, 0))

    grid_spec = pltpu.PrefetchScalarGridSpec(
        num_scalar_prefetch=1,
        grid=(db, n_pages // g),
        in_specs=[vec, vec, vec, pl.BlockSpec((1, n_pages, rows), lambda b, jg, pt: (b, 0, 0))]
                 + [page_spec(i) for i in range(g)] + [page_spec(i) for i in range(g)],
        out_specs=vec,
        scratch_shapes=[pltpu.VMEM((heads, 1), F32), pltpu.VMEM((heads, 1), F32),
                        pltpu.VMEM((heads, HEAD_DIM), F32)],
    )
    return pl.pallas_call(
        functools.partial(_fox_sample_kernel, heads=heads, pages_per_step=g),
        grid_spec=grid_spec,
        out_shape=jax.ShapeDtypeStruct((db, heads, HEAD_DIM), F32),
        compiler_params=_params(("parallel", "arbitrary"), 48),
        name="fox_sample",
    )(pt_flat, q3, kn3, vn3, bias, *([cache_k4] * g), *([cache_v4] * g))


RING = 3


def _fox_sample_ring_kernel(pt_ref, q_ref, kn_ref, vn_ref, bias_ref, k_hbm, v_hbm, o_ref,
                            kbuf, vbuf, sem, m_ref, l_ref, acc_ref, *, heads, g, layer):
    jg = pl.program_id(1)
    n_j = pl.num_programs(1)
    step = pl.program_id(0) * n_j + jg
    n_steps = pl.num_programs(0) * n_j
    rows = kbuf.shape[1] // g

    def copies(s, slot):
        out = []
        for i in range(g):
            page = pt_ref[s * g + i]
            dst = pl.ds(i * rows, rows)
            out.append(pltpu.make_async_copy(k_hbm.at[layer, page], kbuf.at[slot, dst], sem.at[slot, 0]))
            out.append(pltpu.make_async_copy(v_hbm.at[layer, page], vbuf.at[slot, dst], sem.at[slot, 1]))
        return out

    @pl.when(step == 0)
    def _():
        for s in range(RING - 1):
            @pl.when(s < n_steps)
            def _():
                for c in copies(s, s % RING):
                    c.start()

    ahead = step + RING - 1

    @pl.when(ahead < n_steps)
    def _():
        for c in copies(ahead, ahead % RING):
            c.start()

    slot = step % RING
    for c in copies(step, slot):
        c.wait()

    q = q_ref[0] * (HEAD_DIM ** -0.5)

    @pl.when(jg == 0)
    def _():
        m_ref[...] = jnp.sum(q * kn_ref[0], axis=-1, keepdims=True)
        l_ref[...] = jnp.ones_like(l_ref)
        acc_ref[...] = vn_ref[0]

    sub = lax.broadcasted_iota(jnp.int32, (heads, rows), 0)
    lane = lax.broadcasted_iota(jnp.int32, (heads, rows), 1)
    own_head = (lane % heads) == sub
    s_all = []
    for i in range(g):
        sg = _dot3(q, kbuf[slot, pl.ds(i * rows, rows), :], _NT)
        sg = sg + bias_ref[0, pl.ds(jg * g + i, 1), :]
        s_all.append(jnp.where(own_head, sg, NEG_INF))
    m_prev = m_ref[...]
    m_new = m_prev
    for sg in s_all:
        m_new = jnp.maximum(m_new, jnp.max(sg, axis=-1, keepdims=True))
    alpha = jnp.exp(m_prev - m_new)
    l_new = alpha * l_ref[...]
    acc = alpha * acc_ref[...]
    for i, sg in enumerate(s_all):
        p = jnp.exp(sg - m_new)
        l_new = l_new + jnp.sum(p, axis=-1, keepdims=True)
        acc = acc + _dot3(p, vbuf[slot, pl.ds(i * rows, rows), :])
    l_ref[...] = l_new
    acc_ref[...] = acc
    m_ref[...] = m_new

    @pl.when(jg == n_j - 1)
    def _():
        o_ref[0] = (acc_ref[...] / l_ref[...]).astype(o_ref.dtype)


def _fox_sample_ring(q3, kn3, vn3, bias, cache_k4, cache_v4, pt_flat, layer, n_pages):
    db, heads, _ = q3.shape
    rows = cache_k4.shape[2]
    g = _pick(n_pages, (8, 4, 2, 1))
    vec = pl.BlockSpec((1, heads, HEAD_DIM), lambda b, jg, pt: (b, 0, 0))
    hbm = pl.BlockSpec(memory_space=pl.ANY)
    grid_spec = pltpu.PrefetchScalarGridSpec(
        num_scalar_prefetch=1,
        grid=(db, n_pages // g),
        in_specs=[vec, vec, vec, pl.BlockSpec((1, n_pages, rows), lambda b, jg, pt: (b, 0, 0)), hbm, hbm],
        out_specs=vec,
        scratch_shapes=[pltpu.VMEM((RING, g * rows, HEAD_DIM), F32), pltpu.VMEM((RING, g * rows, HEAD_DIM), F32),
                        pltpu.SemaphoreType.DMA((RING, 2)),
                        pltpu.VMEM((heads, 1), F32), pltpu.VMEM((heads, 1), F32), pltpu.VMEM((heads, HEAD_DIM), F32)],
    )
    return pl.pallas_call(
        functools.partial(_fox_sample_ring_kernel, heads=heads, g=g, layer=layer),
        grid_spec=grid_spec,
        out_shape=jax.ShapeDtypeStruct((db, heads, HEAD_DIM), F32),
        compiler_params=_params(("arbitrary", "arbitrary"), 48),
        name="fox_sample",
    )(pt_flat, q3, kn3, vn3, bias, cache_k4, cache_v4)


def _route(logits, n_groups, per_group):
    tm = logits.shape[0]
    lane = lax.broadcasted_iota(jnp.int32, (tm, LANES), 1).astype(F32)
    big = float(4 * LANES)
    first_lane_of = lambda hit: jnp.min(jnp.where(hit, lane, big), axis=-1, keepdims=True)
    gl = jnp.where(lane < n_groups, logits, NEG_INF)
    gexp = jnp.exp(gl - jnp.max(gl, axis=-1, keepdims=True))
    p_grp = gexp / jnp.sum(gexp, axis=-1, keepdims=True)
    g_val = jnp.max(p_grp, axis=-1, keepdims=True)
    g_idx = first_lane_of(p_grp == g_val)
    lo = n_groups + g_idx * per_group
    in_grp = (lane >= lo) & (lane < lo + per_group)
    el = jnp.where(in_grp, logits, NEG_INF)
    eexp = jnp.exp(el - jnp.max(el, axis=-1, keepdims=True))
    p_exp = eexp / jnp.sum(eexp, axis=-1, keepdims=True)
    pe = jnp.where(in_grp, p_exp, -1.0)
    v1 = jnp.max(pe, axis=-1, keepdims=True)
    i1 = first_lane_of(pe == v1)
    pe2 = jnp.where(lane == i1, -1.0, pe)
    v2 = jnp.max(pe2, axis=-1, keepdims=True)
    i2 = first_lane_of(pe2 == v2)
    den = v1 + v2
    return jnp.where(lane == 0, i1 - n_groups,
                     jnp.where(lane == 1, i2 - n_groups,
                               jnp.where(lane == 2, g_val * v1 / den,
                                         jnp.where(lane == 3, g_val * v2 / den, 0.0))))


def _store_row_slabs(x, slab_ref):
    tm, n = x.shape[0], x.shape[1] // LANES
    for s in range(n):
        slab_ref[pl.ds(s, tm, stride=n), :] = x[:, s * LANES:(s + 1) * LANES]


def _load_row_slabs(slab_ref, tm, n, pitch):
    return jnp.concatenate([slab_ref[pl.ds(s, tm, stride=pitch), :] for s in range(n)], axis=1)


def _gather_pitch(n):
    return n + 8


def _residual_norm_route(x, y, nw, wr, br, n_groups, per_group):
    x1 = x + y
    ms = jnp.mean(x1 * x1, axis=-1, keepdims=True)
    xn = x1 * lax.rsqrt(ms + EPS) * nw
    return x1, xn, _route(_dot3(xn, wr) + br, n_groups, per_group)


def _outproj_kernel(or_ref, of_ref, x_ref, wo_ref, nw_ref, wr_ref, br_ref,
                    x1_ref, xg_ref, route_ref, *, w_ret, n_groups, per_group):
    y = jnp.dot(or_ref[...], wo_ref[:w_ret, :], preferred_element_type=F32)
    y = y + jnp.dot(of_ref[...], wo_ref[w_ret:, :], preferred_element_type=F32)
    x1_ref[...], xn, route_ref[...] = _residual_norm_route(
        x_ref[...], y, nw_ref[...], wr_ref[...], br_ref[...], n_groups, per_group)
    _store_row_slabs(xn, xg_ref)


def _outproj(o_r, o_f, x, w_out_b, nw, w_r, b_r, n_groups, per_group):
    t, d = x.shape
    w_ret = o_r.shape[1]
    w_fox = o_f.shape[1]
    tm = _pick(t, (320, 256, 128, 64, 8))
    row = lambda i: (i, 0)
    fixed = lambda i: (0, 0)
    return pl.pallas_call(
        functools.partial(_outproj_kernel, w_ret=w_ret, n_groups=n_groups, per_group=per_group),
        grid=(t // tm,),
        in_specs=[pl.BlockSpec((tm, w_ret), row), pl.BlockSpec((tm, w_fox), row), pl.BlockSpec((tm, d), row),
                  pl.BlockSpec((w_ret + w_fox, d), fixed), pl.BlockSpec((1, d), fixed),
                  pl.BlockSpec((d, LANES), fixed), pl.BlockSpec((1, LANES), fixed)],
        out_specs=[pl.BlockSpec((tm, d), row), pl.BlockSpec((tm * (d // LANES), LANES), row),
                   pl.BlockSpec((tm, LANES), row)],
        out_shape=[jax.ShapeDtypeStruct((t, d), F32), jax.ShapeDtypeStruct((t * (d // LANES), LANES), F32),
                   jax.ShapeDtypeStruct((t, LANES), F32)],
        compiler_params=_params(("parallel",), 56),
        name="outproj_router",
    )(o_r, o_f, x, w_out_b, nw.reshape(1, d), w_r, b_r)


def _rms(x, nw):
    ms = jnp.mean(x * x, axis=-1, keepdims=True)
    return x * lax.rsqrt(ms + EPS) * nw


def _sample_proj_kernel(x_ref, nw_ref, w_ref, o_ref):
    o_ref[...] = _dot3(_rms(x_ref[...], nw_ref[...]), w_ref[0], _NT)


def _sample_logf_kernel(x_ref, nw_ref, w_ref, b_ref, o_ref):
    z = _dot3(_rms(x_ref[...], nw_ref[...]), w_ref[0], _NT) + b_ref[...]
    o_ref[...] = _log_sigmoid(z)


def _sample_proj(x, nw, w_t, layer, s_len, db, n_cols):
    d = x.shape[1]
    tn = _pick(n_cols, (512, 256, 128))
    return pl.pallas_call(
        _sample_proj_kernel,
        grid=(n_cols // tn,),
        in_specs=[pl.BlockSpec((db, d), lambda j: (s_len // db, 0)),
                  pl.BlockSpec((1, d), lambda j: (0, 0)),
                  pl.BlockSpec((1, tn, d), lambda j: (layer, j, 0))],
        out_specs=pl.BlockSpec((db, tn), lambda j: (0, j)),
        out_shape=jax.ShapeDtypeStruct((db, n_cols), F32),
        compiler_params=_params(("parallel",), 48),
        name="sample_proj",
    )(x, nw.reshape(1, d), w_t)


def _sample_logf(x, nw, w_t, layer, row0, bias, s_len, db):
    d = x.shape[1]
    n_out = bias.shape[1]
    return pl.pallas_call(
        _sample_logf_kernel,
        grid=(1,),
        in_specs=[pl.BlockSpec((db, d), lambda j: (s_len // db, 0)),
                  pl.BlockSpec((1, d), lambda j: (0, 0)),
                  pl.BlockSpec((1, n_out, d), lambda j: (layer, row0 // n_out, 0)),
                  pl.BlockSpec((1, n_out), lambda j: (0, 0))],
        out_specs=pl.BlockSpec((db, n_out), lambda j: (0, 0)),
        out_shape=jax.ShapeDtypeStruct((db, n_out), F32),
        compiler_params=_params(("arbitrary",), 40),
        name="sample_logf",
    )(x, nw.reshape(1, d), w_t, bias)


def _sample_outproj_kernel(cat_ref, w_ref, y_ref):
    y_ref[...] = _dot3(cat_ref[...], w_ref[0])


def _sample_outproj(cat, w_out, layer):
    db, width = cat.shape
    d = w_out.shape[2]
    tn = _pick(d, (512, 256, 128))
    return pl.pallas_call(
        _sample_outproj_kernel,
        grid=(d // tn,),
        in_specs=[pl.BlockSpec((db, width), lambda j: (0, 0)),
                  pl.BlockSpec((1, width, tn), lambda j: (layer, 0, j))],
        out_specs=pl.BlockSpec((db, tn), lambda j: (0, j)),
        out_shape=jax.ShapeDtypeStruct((db, d), F32),
        compiler_params=_params(("parallel",), 48),
        name="sample_outproj",
    )(cat, w_out)


def _sample_router_kernel(y_ref, x_ref, nw_ref, wr_ref, br_ref, x1_in, xg_in, route_in,
                          x1_ref, xg_ref, route_ref, *, n_groups, per_group):
    del x1_in, xg_in, route_in
    x1_ref[...], xn, route_ref[...] = _residual_norm_route(
        x_ref[...], y_ref[...], nw_ref[...], wr_ref[...], br_ref[...], n_groups, per_group)
    _store_row_slabs(xn, xg_ref)


def _sample_router(y, x, nw, w_r, b_r, x1, xg, route, s_len, n_groups, per_group):
    db, d = y.shape
    blk = s_len // db
    fixed = lambda i: (0, 0)
    rows = lambda i: (blk, 0)
    hbm = pl.BlockSpec(memory_space=pl.ANY)
    return pl.pallas_call(
        functools.partial(_sample_router_kernel, n_groups=n_groups, per_group=per_group),
        grid=(1,),
        in_specs=[pl.BlockSpec((db, d), fixed), pl.BlockSpec((db, d), rows), pl.BlockSpec((1, d), fixed),
                  pl.BlockSpec((d, LANES), fixed), pl.BlockSpec((1, LANES), fixed), hbm, hbm, hbm],
        out_specs=[pl.BlockSpec((db, d), rows), pl.BlockSpec((db * (d // LANES), LANES), rows),
                   pl.BlockSpec((db, LANES), rows)],
        out_shape=[jax.ShapeDtypeStruct(x1.shape, F32), jax.ShapeDtypeStruct(xg.shape, F32),
                   jax.ShapeDtypeStruct(route.shape, F32)],
        input_output_aliases={5: 0, 6: 1, 7: 2},
        compiler_params=_params(("arbitrary",), 40),
        name="sample_router",
    )(y, x, nw.reshape(1, d), w_r, b_r, x1, xg, route)


def _start_row_gather(src_of, hbm, buf, sem, rows, n, unrolled=False):
    pitch = _gather_pitch(n)

    def start(r, priority):
        src = pl.multiple_of(src_of(r) * n, n)
        dst = r * pitch if isinstance(r, int) else pl.multiple_of(r * pitch, 8)
        pltpu.make_async_copy(hbm.at[pl.ds(src, n)], buf.at[pl.ds(dst, n)], sem).start(priority=priority)

    if unrolled:
        for r in range(rows):
            start(r, r % 2)
        return

    def pair(h, carry):
        for u in range(2):
            start(2 * h + u, u)
        return carry

    lax.fori_loop(0, rows // 2, pair, 0)


def _wait_row_gather(hbm, buf, sem, rows, n):
    pltpu.make_async_copy(hbm.at[pl.ds(0, rows * n)], buf.at[pl.ds(0, rows * n)], sem).wait()


def _expert_kernel(te_ref, nu_ref, src_ref, x_hbm, w1_ref, w3_ref, w2_ref, y_ref, xbuf, sem):
    del te_ref
    i = pl.program_id(0)
    n_used = nu_ref[0]
    n = w1_ref.shape[2] // LANES
    tm = y_ref.shape[0] // n

    def fetch(tile, slot, unrolled):
        _start_row_gather(lambda r: src_ref[tile * tm + r], x_hbm, xbuf.at[slot], sem.at[slot], tm, n, unrolled)

    @pl.when(i == 0)
    def _():
        fetch(0, 0, False)

    @pl.when(i == n_used)
    def _():
        _wait_row_gather(x_hbm, xbuf.at[i % 2], sem.at[i % 2], tm, n)

    @pl.when(i < n_used)
    def _():
        slot = i % 2
        fetch(i + 1, 1 - slot, True)
        _wait_row_gather(x_hbm, xbuf.at[slot], sem.at[slot], tm, n)
        x = _load_row_slabs(xbuf.at[slot], tm, n, _gather_pitch(n)).astype(BF16)
        a = jnp.dot(x, w1_ref[0, 0], preferred_element_type=F32)
        b = jnp.dot(x, w3_ref[0, 0], preferred_element_type=F32)
        hmid = (_silu(a) * b).astype(BF16)
        _store_row_slabs(jnp.dot(hmid, w2_ref[0, 0], preferred_element_type=F32), y_ref)

    @pl.when(i >= n_used)
    def _():
        y_ref[...] = jnp.zeros_like(y_ref)


def _experts(xg, src, w1b, w3b, w2b, layer, tile_expert, n_used, tm, n_tiles):
    d, de = w1b.shape[2], w1b.shape[3]
    n = d // LANES

    def wsel(i, te, nu, s):
        return (layer, te[i], 0, 0)

    grid_spec = pltpu.PrefetchScalarGridSpec(
        num_scalar_prefetch=3,
        grid=(n_tiles,),
        in_specs=[pl.BlockSpec(memory_space=pl.ANY), pl.BlockSpec((1, 1, d, de), wsel),
                  pl.BlockSpec((1, 1, d, de), wsel), pl.BlockSpec((1, 1, de, d), wsel)],
        out_specs=pl.BlockSpec((tm * n, LANES), lambda i, te, nu, s: (i, 0)),
        scratch_shapes=[pltpu.VMEM((2, tm * _gather_pitch(n), LANES), F32), pltpu.SemaphoreType.DMA((2,))],
    )
    return pl.pallas_call(
        _expert_kernel,
        grid_spec=grid_spec,
        out_shape=jax.ShapeDtypeStruct((n_tiles * tm * n, LANES), F32),
        compiler_params=_params(("arbitrary",), 56),
        name="moe_experts",
    )(tile_expert, n_used, src, xg, w1b, w3b, w2b)


def _combine_kernel(pos_ref, x_ref, route_ref, nw_ref, y_hbm, *refs, rows, split_at):
    o_refs, (ybuf, sem) = refs[:-2], refs[-2:]
    i = pl.program_id(0)
    n = x_ref.shape[1] // LANES
    pitch = _gather_pitch(n)

    def fetch(tile, slot):
        for k in range(TOP_K_IN_GROUP):
            _start_row_gather(lambda r: pos_ref[TOP_K_IN_GROUP * (tile * rows + r) + k], y_hbm,
                              ybuf.at[slot, k], sem.at[slot, k], rows, n)

    @pl.when(i == 0)
    def _():
        fetch(0, 0)

    @pl.when(i + 1 < pl.num_programs(0))
    def _():
        fetch(i + 1, (i + 1) % 2)

    slot = i % 2
    for k in range(TOP_K_IN_GROUP):
        _wait_row_gather(y_hbm, ybuf.at[slot, k], sem.at[slot, k], rows, n)
    route = route_ref[...]
    wt0 = jnp.broadcast_to(route[:, 2:3], (rows, LANES))
    wt1 = jnp.broadcast_to(route[:, 3:4], (rows, LANES))
    moe = jnp.concatenate([wt0 * ybuf[slot, 0, pl.ds(s, rows, stride=pitch), :]
                           + wt1 * ybuf[slot, 1, pl.ds(s, rows, stride=pitch), :] for s in range(n)], axis=1)
    x2 = x_ref[...] + moe
    ms = jnp.mean(x2 * x2, axis=-1, keepdims=True)
    xn = x2 * lax.rsqrt(ms + EPS) * nw_ref[...]
    if split_at is None:
        x2_ref, xn_ref = o_refs
        x2_ref[...] = x2
        xn_ref[...] = xn.astype(xn_ref.dtype)
    else:
        yp_ref, ys_ref = o_refs

        @pl.when(i < split_at)
        def _():
            yp_ref[...] = xn

        @pl.when(i >= split_at)
        def _():
            ys_ref[...] = xn


def _combine(x1, route, y_sorted, pos_flat, nw, split_rows=None):
    t, d = x1.shape
    rows = _pick(t, (128, 64, 8))
    row = lambda i, p: (i, 0)
    if split_rows is None:
        split_at = None
        out_specs = [pl.BlockSpec((rows, d), row), pl.BlockSpec((rows, d), row)]
        out_shape = [jax.ShapeDtypeStruct((t, d), F32), jax.ShapeDtypeStruct((t, d), BF16)]
    else:
        split_at = split_rows // rows
        out_specs = [pl.BlockSpec((rows, d), lambda i, p: (jnp.minimum(i, split_at - 1), 0)),
                     pl.BlockSpec((rows, d), lambda i, p: (jnp.maximum(i - split_at, 0), 0))]
        out_shape = [jax.ShapeDtypeStruct((split_rows, d), F32), jax.ShapeDtypeStruct((t - split_rows, d), F32)]
    grid_spec = pltpu.PrefetchScalarGridSpec(
        num_scalar_prefetch=1,
        grid=(t // rows,),
        in_specs=[pl.BlockSpec((rows, d), row), pl.BlockSpec((rows, LANES), row),
                  pl.BlockSpec((1, d), lambda i, p: (0, 0)), pl.BlockSpec(memory_space=pl.ANY)],
        out_specs=out_specs,
        scratch_shapes=[pltpu.VMEM((2, TOP_K_IN_GROUP, rows * _gather_pitch(d // LANES), LANES), F32),
                        pltpu.SemaphoreType.DMA((2, TOP_K_IN_GROUP))],
    )
    return pl.pallas_call(
        functools.partial(_combine_kernel, rows=rows, split_at=split_at),
        grid_spec=grid_spec,
        out_shape=out_shape,
        compiler_params=_params(("arbitrary",), 40),
        name="moe_combine",
    )(pos_flat, x1, route, nw.reshape(1, d), y_sorted)


def _routing_tables(e_id, n_experts, tm, n_tiles):
    flat = e_id.reshape(-1)
    onehot = (flat[:, None] == jnp.arange(n_experts, dtype=jnp.int32)[None, :]).astype(jnp.int32)
    csum = jnp.cumsum(onehot, axis=0)
    rank = jnp.sum((csum - onehot) * onehot, axis=1)
    counts = csum[-1]
    padded = ((counts + tm - 1) // tm) * tm
    ends = jnp.cumsum(padded)
    starts = ends - padded
    pos = (jnp.sum(onehot * starts[None, :], axis=1) + rank).astype(jnp.int32)
    token = jnp.arange(flat.shape[0], dtype=jnp.int32) // TOP_K_IN_GROUP
    src = jnp.zeros((n_tiles * tm,), jnp.int32).at[pos].set(token)
    tile_start = jnp.arange(n_tiles, dtype=jnp.int32) * tm
    n_used = (ends[-1] // tm).astype(jnp.int32)
    tile_expert = jnp.sum((tile_start[:, None] >= ends[None, :]).astype(jnp.int32), axis=1)
    last_expert = jnp.sum((jnp.maximum(ends[-1] - 1, 0) >= ends).astype(jnp.int32))
    tile_expert = jnp.minimum(tile_expert, last_expert).astype(jnp.int32)
    return src, pos, tile_expert, n_used.reshape(1)


def kernel(x_prompt, x_sample, cache_k, cache_v, cache_logf, state_ret, page_table, norm_mix_w, w_in, b_fgt,
           ret_gn_w, w_out, norm_ffn_w, w_group, b_group, w_expert, b_expert, w1, w3, w2, norm_final_w):
    assert x_prompt.shape[0] == 1 and x_sample.shape[1] == 1
    s_len, d = x_prompt.shape[1], x_prompt.shape[2]
    db = x_sample.shape[0]
    depth = w_in.shape[0]
    n_pool, page, h_fox = cache_k.shape[1], cache_k.shape[2], cache_k.shape[3]
    h_ret = state_ret.shape[2]
    w_ret, w_fox = h_ret * HEAD_DIM, h_fox * HEAD_DIM
    n_pages = page_table.shape[1]
    past_len = n_pages * page
    n_groups, n_experts = w_group.shape[-1], w_expert.shape[-1]
    per_group = n_experts // n_groups
    t = s_len + db
    chunk = 128
    tq_f = _pick(s_len, (512, 256, 128))
    tm_e = 256
    n_tiles = (TOP_K_IN_GROUP * t + n_experts * (tm_e - 1) + tm_e - 1) // tm_e + 1

    inv_freq = ROPE_BASE ** (-jnp.arange(0, HEAD_DIM, 2, dtype=F32) / HEAD_DIM)
    posv = jnp.concatenate([jnp.arange(s_len), jnp.full((db,), past_len)]).astype(F32)
    ang = posv[:, None] * inv_freq[None, :]
    cos2 = jnp.concatenate([jnp.cos(ang), jnp.cos(ang)], axis=-1)
    sin2 = jnp.concatenate([-jnp.sin(ang), jnp.sin(ang)], axis=-1)

    cache_k4 = cache_k.reshape(depth, n_pool, page * h_fox, HEAD_DIM)
    cache_v4 = cache_v.reshape(depth, n_pool, page * h_fox, HEAD_DIM)
    pt_flat = page_table.reshape(-1)
    assert w_ret == w_fox
    w_t = jnp.swapaxes(w_in, 1, 2)

    w1b, w3b, w2b = w1.astype(BF16), w3.astype(BF16), w2.astype(BF16)

    x = jnp.concatenate([x_prompt[0], x_sample[:, 0]], axis=0)
    xn = _rmsnorm(x, norm_mix_w[0], BF16)

    k_p, v_p, lf_p, st_p, k_s, v_s, lf_s, st_s = [], [], [], [], [], [], [], []
    for l in range(depth):
        qk = _proj_rope(xn, w_t, l, cos2, sin2, w_ret, s_len)
        vg, = _proj(xn, w_t, l, 2, 2, (F32,), w_ret, s_len, name="proj_vg")
        c0 = 4 * w_ret
        b_f = b_fgt[l].reshape(1, h_fox)
        logf = _proj_logf(xn, w_t, l, c0 + 3 * w_fox, b_f, s_len)
        o_r_p, s_fin = _retention_prompt(qk, vg, ret_gn_w[l], s_len, w_ret, chunk)
        c2 = jnp.cumsum(logf, axis=0) * LOG2E
        kf, ka = _proj_k_aug(xn, w_t, l, 5, c2, w_fox)
        vf, = _proj(xn, w_t, l, 6, 1, (F32,), w_ret, s_len, name="proj_v")
        qa = _proj_t(xn, w_t, l, c0, h_fox, tq_f, s_len, c_t=c2.T, scale=HEAD_DIM ** -0.5 * LOG2E, name="proj_qt")
        v1 = _proj_t(xn, w_t, l, c0 + 2 * w_fox, h_fox, tq_f, s_len, name="proj_vt")
        o_f_p = _fox_prompt_t(qa, ka, v1, s_len, w_fox)

        nw_mix = norm_mix_w[l]
        p_s = _sample_proj(x, nw_mix, w_t, l, s_len, db, c0 + 3 * w_fox)
        logf_s = _sample_logf(x, nw_mix, w_t, l, c0 + 3 * w_fox, b_f, s_len, db)
        heads3 = lambda w0, h: p_s[:, w0:w0 + h * HEAD_DIM].reshape(db, h, HEAD_DIM)
        o_r_s, s_new = _retention_sample(heads3(0, h_ret), heads3(w_ret, h_ret), heads3(2 * w_ret, h_ret),
                                         heads3(3 * w_ret, h_ret), cos2[s_len:s_len + 1], sin2[s_len:s_len + 1],
                                         ret_gn_w[l], state_ret, l)
        lf_past = cache_logf[l][page_table].reshape(db, past_len, h_fox)
        suffix = jnp.flip(jnp.cumsum(jnp.flip(lf_past, axis=1), axis=1), axis=1) - lf_past
        bias = (suffix + logf_s[:, None, :]).reshape(db, n_pages, page * h_fox)
        o_f_s = _fox_sample_ring(heads3(c0, h_fox), heads3(c0 + w_fox, h_fox), heads3(c0 + 2 * w_fox, h_fox), bias,
                            cache_k4, cache_v4, pt_flat, l, n_pages)
        cat_s = jnp.concatenate([o_r_s.reshape(db, w_ret), o_f_s.reshape(db, w_fox)], axis=1)

        o_r = jnp.concatenate([o_r_p, cat_s[:, :w_ret].astype(BF16)], axis=0)
        o_f = jnp.concatenate([o_f_p, cat_s[:, w_ret:].astype(BF16)], axis=0)

        w_r = jnp.zeros((d, LANES), F32).at[:, :n_groups].set(w_group[l]).at[:, n_groups:n_groups + n_experts].set(w_expert[l])
        b_r = jnp.zeros((1, LANES), F32).at[0, :n_groups].set(b_group[l]).at[0, n_groups:n_groups + n_experts].set(b_expert[l])
        x1, xg, route = _outproj(o_r, o_f, x, w_out[l].astype(BF16), norm_ffn_w[l], w_r, b_r, n_groups, per_group)
        y_s = _sample_outproj(cat_s, w_out, l)
        x1, xg, route = _sample_router(y_s, x, norm_ffn_w[l], w_r, b_r, x1, xg, route, s_len, n_groups, per_group)

        e_id = route[:, :TOP_K_IN_GROUP].astype(jnp.int32)
        src, pos, tile_expert, n_used = _routing_tables(e_id, n_experts, tm_e, n_tiles)
        y_sorted = _experts(xg, src, w1b, w3b, w2b, l, tile_expert, n_used, tm_e, n_tiles)
        if l < depth - 1:
            x, xn = _combine(x1, route, y_sorted, pos, norm_mix_w[l + 1])
        else:
            y_prompt, y_sample = _combine(x1, route, y_sorted, pos, norm_final_w, split_rows=s_len)

        k_p.append(kf.reshape(1, s_len, h_fox, HEAD_DIM))
        v_p.append(vf.reshape(1, s_len, h_fox, HEAD_DIM))
        lf_p.append(logf.reshape(1, s_len, h_fox))
        st_p.append(s_fin.reshape(1, h_ret, HEAD_DIM, HEAD_DIM))
        k_s.append(heads3(c0 + w_fox, h_fox).reshape(db, 1, h_fox, HEAD_DIM))
        v_s.append(heads3(c0 + 2 * w_fox, h_fox).reshape(db, 1, h_fox, HEAD_DIM))
        lf_s.append(logf_s.reshape(db, 1, h_fox))
        st_s.append(s_new)

    return (y_prompt.reshape(1, s_len, d), y_sample.reshape(db, 1, d), jnp.stack(k_p), jnp.stack(v_p), jnp.stack(lf_p), jnp.stack(st_p),
            jnp.stack(k_s), jnp.stack(v_s), jnp.stack(lf_s), jnp.stack(st_s))
```
